```python
import math
import jax, jax.numpy as jnp
from jax import lax
import numpy as np

D_MODEL = 2048
BATCH = 2
SEQ = 4096
DEPTH = 2

HEAD_DIM = 128
ROPE_THETA = 10000.0
Q_BLK = 128
NSA_HEADS = 8
NSA_KV_HEADS = 2
NSA_GROUP = NSA_HEADS // NSA_KV_HEADS
CMP_LEN = 32
CMP_STRIDE = 16
SEL_LEN = 64
SEL_TOPK = 16
WIN = 512
SEL_FORCE = 1.0e4
SB_HEADS = 8
DIFF_HEADS = 8
LAMBDA_STD = 0.1
N_EVEN = (DEPTH + 1) // 2
N_ODD = DEPTH // 2
DEEPNORM_ALPHA = (2.0 * DEPTH) ** 0.25
DEEPNORM_BETA = (8.0 * DEPTH) ** -0.25
LN_EPS = 1e-5
RMS_EPS = 1e-5
A_Q = NSA_HEADS * HEAD_DIM
A_KV = NSA_KV_HEADS * HEAD_DIM
A_GATES = 3 * NSA_HEADS
B_W = SB_HEADS * HEAD_DIM
EVEN_SIZES = (A_Q, A_KV, A_KV, A_KV, A_KV, A_KV, A_KV, A_GATES, A_Q, B_W, B_W, B_W, B_W)
EVEN_IN = A_Q + 6 * A_KV + A_GATES + A_Q + 4 * B_W
EVEN_OUT = A_Q + B_W
C_W = DIFF_HEADS * 2 * HEAD_DIM
ODD_SIZES = (C_W, C_W, C_W, C_W)
ODD_IN = 4 * C_W
ODD_OUT = C_W

kernel_name = 'hybrid_nsa_stickbreak_diffattn_deepnorm'


def _split(h, sizes):
    offs = np.cumsum(np.array(sizes))[:-1].tolist()
    return jnp.split(h, offs, axis=-1)


def _rope_tables(seq):
    pos = jnp.arange(seq, dtype=jnp.float32)
    inv = ROPE_THETA ** (-jnp.arange(0, HEAD_DIM, 2, dtype=jnp.float32) / HEAD_DIM)
    ang = pos[:, None] * inv[None, :]
    return jnp.cos(ang), jnp.sin(ang)


def _rope(x, cos, sin):
    xf = x.astype(jnp.float32)
    x1, x2 = jnp.split(xf, 2, axis=-1)
    c = cos[None, :, None, :]
    s = sin[None, :, None, :]
    return jnp.concatenate([x1 * c - x2 * s, x2 * c + x1 * s], axis=-1).astype(x.dtype)


def _layer_norm(x, g, b):
    xf = x.astype(jnp.float32)
    mu = jnp.mean(xf, axis=-1, keepdims=True)
    var = jnp.mean(jnp.square(xf - mu), axis=-1, keepdims=True)
    y = (xf - mu) * lax.rsqrt(var + LN_EPS) * g.astype(jnp.float32) + b.astype(jnp.float32)
    return y.astype(x.dtype)


def _masked_softmax(s, mask):
    s = jnp.where(mask, s.astype(jnp.float32), -jnp.inf)
    m = jnp.max(s, axis=-1, keepdims=True)
    m = jnp.where(jnp.isfinite(m), m, 0.0)
    e = jnp.exp(s - m)
    den = jnp.sum(e, axis=-1, keepdims=True)
    return e / jnp.where(den > 0.0, den, 1.0)


def _nsa_compress(kv, pe, w1, w2):
    b, s, hk, d = kv.shape
    n_sub = CMP_LEN // CMP_STRIDE
    chunks = kv.reshape(b, s // CMP_STRIDE, CMP_STRIDE, hk, d)
    n_cmp = s // CMP_STRIDE - n_sub + 1
    blocks = jnp.concatenate([chunks[:, m:m + n_cmp] for m in range(n_sub)], axis=2)
    blocks = blocks + pe[None, None, :, None, :]
    flat = blocks.transpose(0, 1, 3, 2, 4).reshape(b, n_cmp, hk, CMP_LEN * d)
    return jax.nn.silu(flat @ w1) @ w2


def _nsa(q, kc, vc, ks, vs, kw, vw, gates, cos, sin, pe_k, pe_v, w1k, w2k, w1v, w2v):
    b, s = q.shape[0], q.shape[1]
    hk, g, d = NSA_KV_HEADS, NSA_GROUP, HEAD_DIM
    scale = HEAD_DIM ** -0.5
    t = jnp.arange(s)
    k_cmp = _nsa_compress(kc, pe_k, w1k, w2k)
    v_cmp = _nsa_compress(vc, pe_v, w1v, w2v)
    n_cmp = k_cmp.shape[1]
    qg = q.reshape(b, s, hk, g, d)
    sc = jnp.einsum('bthgd,bnhd->bhgtn', qg, k_cmp).astype(jnp.float32) * scale
    blk_end = jnp.arange(n_cmp) * CMP_STRIDE + CMP_LEN - 1
    p_cmp = _masked_softmax(sc, blk_end[None, :] <= t[:, None])
    o_cmp = jnp.einsum('bhgtn,bnhd->bthgd', p_cmp.astype(v_cmp.dtype), v_cmp)
    n_sel_blk = s // SEL_LEN
    c_start = jnp.arange(n_cmp) * CMP_STRIDE
    s_start = jnp.arange(n_sel_blk) * SEL_LEN
    overlap = ((c_start[:, None] < s_start[None, :] + SEL_LEN)
               & (c_start[:, None] + CMP_LEN > s_start[None, :])).astype(jnp.float32)
    imp = jnp.einsum('bhgtn,nj->bhtj', p_cmp, overlap)
    j_cur = t // SEL_LEN
    jj = jnp.arange(n_sel_blk)
    valid = jj[None, :] <= j_cur[:, None]
    forced = (jj[None, :] == 0) | (jj[None, :] == j_cur[:, None]) | (jj[None, :] == j_cur[:, None] - 1)
    score = jnp.where(forced, SEL_FORCE, jnp.where(valid, imp, -SEL_FORCE))
    n_sel = min(SEL_TOPK, n_sel_blk)
    _, idx = lax.top_k(score, n_sel)
    qr = _rope(q, cos, sin).reshape(b, s, hk, g, d).transpose(0, 2, 3, 1, 4)
    ks_b = _rope(ks, cos, sin).transpose(0, 2, 1, 3).reshape(b, hk, n_sel_blk, SEL_LEN, d)
    vs_b = vs.transpose(0, 2, 1, 3).reshape(b, hk, n_sel_blk, SEL_LEN, d)
    pad = ((0, 0), (0, 0), (WIN, 0), (0, 0))
    kw_p = jnp.pad(_rope(kw, cos, sin).transpose(0, 2, 1, 3), pad)
    vw_p = jnp.pad(vw.transpose(0, 2, 1, 3), pad)
    gather = jax.vmap(jax.vmap(lambda tab, ix: tab[ix]))

    def block(i):
        q0 = i * Q_BLK
        tq = q0 + jnp.arange(Q_BLK)
        qb = lax.dynamic_slice_in_dim(qr, q0, Q_BLK, axis=3)
        ib = lax.dynamic_slice_in_dim(idx, q0, Q_BLK, axis=2)
        kg = gather(ks_b, ib)
        vg = gather(vs_b, ib)
        s_sel = jnp.einsum('bhgqd,bhqnjd->bhgqnj', qb, kg).astype(jnp.float32) * scale
        pos = ib[..., None] * SEL_LEN + jnp.arange(SEL_LEN)
        m_sel = (pos <= tq[None, None, :, None, None]).reshape(b, hk, 1, Q_BLK, n_sel * SEL_LEN)
        p_sel = _masked_softmax(s_sel.reshape(b, hk, g, Q_BLK, n_sel * SEL_LEN), m_sel)
        p_sel = p_sel.reshape(b, hk, g, Q_BLK, n_sel, SEL_LEN).astype(vg.dtype)
        o_sel = jnp.einsum('bhgqnj,bhqnjd->bhgqd', p_sel, vg)
        kwb = lax.dynamic_slice_in_dim(kw_p, q0, WIN + Q_BLK, axis=2)
        vwb = lax.dynamic_slice_in_dim(vw_p, q0, WIN + Q_BLK, axis=2)
        spos = q0 - WIN + jnp.arange(WIN + Q_BLK)
        m_w = (spos[None, :] >= 0) & (spos[None, :] <= tq[:, None]) & (tq[:, None] - spos[None, :] < WIN)
        s_w = jnp.einsum('bhgqd,bhkd->bhgqk', qb, kwb).astype(jnp.float32) * scale
        p_w = _masked_softmax(s_w, m_w).astype(vwb.dtype)
        o_w = jnp.einsum('bhgqk,bhkd->bhgqd', p_w, vwb)
        return o_sel, o_w

    o_sel, o_win = lax.map(block, jnp.arange(s // Q_BLK))
    o_sel = o_sel.transpose(1, 0, 4, 2, 3, 5).reshape(b, s, hk, g, d)
    o_win = o_win.transpose(1, 0, 4, 2, 3, 5).reshape(b, s, hk, g, d)
    gt = jax.nn.sigmoid(gates.astype(jnp.float32)).reshape(b, s, hk, g, 3)
    o = gt[..., 0:1] * o_cmp + gt[..., 1:2] * o_sel + gt[..., 2:3] * o_win
    return o.reshape(b, s, hk * g * d).astype(q.dtype)


def _stick_breaking(q, k, v):
    b, s, h, d = q.shape
    scale = HEAD_DIM ** -0.5
    qh = q.transpose(0, 2, 1, 3)
    kh = k.transpose(0, 2, 1, 3)
    vh = v.transpose(0, 2, 1, 3)
    s_idx = jnp.arange(s)

    def block(i):
        q0 = i * Q_BLK
        tq = q0 + jnp.arange(Q_BLK)
        qb = lax.dynamic_slice_in_dim(qh, q0, Q_BLK, axis=2)
        z = jnp.einsum('bhqd,bhkd->bhqk', qb, kh).astype(jnp.float32) * scale
        mask = s_idx[None, :] < tq[:, None]
        log1m = jnp.where(mask, jax.nn.log_sigmoid(-z), 0.0)
        between = lax.cumsum(log1m, axis=3, reverse=True) - log1m
        a = jnp.where(mask, jnp.exp(jax.nn.log_sigmoid(z) + between), 0.0)
        return jnp.einsum('bhqk,bhkd->bhqd', a.astype(vh.dtype), vh)

    o = lax.map(block, jnp.arange(s // Q_BLK))
    return o.transpose(1, 0, 3, 2, 4).reshape(b, s, h * d)


def _diff_attn(q, k, v, lq1, lk1, lq2, lk2, gn_g, lambda_init, cos, sin):
    b, s = q.shape[0], q.shape[1]
    h, d = DIFF_HEADS, HEAD_DIM
    scale = HEAD_DIM ** -0.5
    qh = _rope(q.reshape(b, s, 2 * h, d), cos, sin).reshape(b, s, h, 2, d).transpose(0, 2, 3, 1, 4)
    kh = _rope(k.reshape(b, s, 2 * h, d), cos, sin).reshape(b, s, h, 2, d).transpose(0, 2, 3, 1, 4)
    vh = v.reshape(b, s, h, 2 * d).transpose(0, 2, 1, 3)
    lam = (jnp.exp(jnp.sum(lq1.astype(jnp.float32) * lk1.astype(jnp.float32)))
           - jnp.exp(jnp.sum(lq2.astype(jnp.float32) * lk2.astype(jnp.float32))) + lambda_init)
    s_idx = jnp.arange(s)

    def block(i):
        q0 = i * Q_BLK
        tq = q0 + jnp.arange(Q_BLK)
        qb = lax.dynamic_slice_in_dim(qh, q0, Q_BLK, axis=3)
        sc = jnp.einsum('bhcqd,bhckd->bhcqk', qb, kh).astype(jnp.float32) * scale
        p = _masked_softmax(sc, s_idx[None, :] <= tq[:, None])
        a = p[:, :, 0] - lam * p[:, :, 1]
        return jnp.einsum('bhqk,bhke->bhqe', a.astype(vh.dtype), vh)

    o = lax.map(block, jnp.arange(s // Q_BLK))
    o = o.transpose(1, 0, 3, 2, 4).reshape(b, s, h, 2 * d).astype(jnp.float32)
    o = o * lax.rsqrt(jnp.mean(jnp.square(o), axis=-1, keepdims=True) + RMS_EPS)
    o = o * gn_g.astype(jnp.float32).reshape(h, 2 * d) * (1.0 - lambda_init)
    return o.reshape(b, s, h * 2 * d).astype(q.dtype)


def _even_layer(x, w_in, pe_k, pe_v, w1k, w2k, w1v, w2v, w_out, ln_g, ln_b, cos, sin):
    b, s, _ = x.shape
    hproj = x @ w_in
    (qa, kc, vc, ks, vs, kw, vw, ga, gate_a, qb, kb, vb, gate_b) = _split(hproj, EVEN_SIZES)
    r = lambda t, hh: t.reshape(b, s, hh, HEAD_DIM)
    o_a = _nsa(r(qa, NSA_HEADS), r(kc, NSA_KV_HEADS), r(vc, NSA_KV_HEADS), r(ks, NSA_KV_HEADS),
               r(vs, NSA_KV_HEADS), r(kw, NSA_KV_HEADS), r(vw, NSA_KV_HEADS), ga, cos, sin,
               pe_k, pe_v, w1k, w2k, w1v, w2v) * jax.nn.silu(gate_a)
    o_b = _stick_breaking(r(qb, SB_HEADS), r(kb, SB_HEADS), r(vb, SB_HEADS)) * jax.nn.silu(gate_b)
    y = jnp.concatenate([o_a, o_b], axis=-1) @ w_out
    return _layer_norm(DEEPNORM_ALPHA * x + y, ln_g, ln_b)


def _odd_layer(x, w_in, lq1, lk1, lq2, lk2, gn_g, w_out, ln_g, ln_b, lambda_init, cos, sin):
    q, k, v, gate = _split(x @ w_in, ODD_SIZES)
    o = _diff_attn(q, k, v, lq1, lk1, lq2, lk2, gn_g, lambda_init, cos, sin) * jax.nn.silu(gate)
    return _layer_norm(DEEPNORM_ALPHA * x + o @ w_out, ln_g, ln_b)


def setup_inputs(seed: int = 0) -> dict:
    key = jax.random.key(seed)
    k = jax.random.split(key, 24)
    nrm = lambda kk, shape, sc: jax.random.normal(kk, shape, jnp.float32) * sc
    d = HEAD_DIM
    return {
        'x': nrm(k[0], (BATCH, SEQ, D_MODEL), 1.0),
        'ev_w_in': nrm(k[1], (N_EVEN, D_MODEL, EVEN_IN), D_MODEL ** -0.5),
        'ev_pe_k': nrm(k[2], (N_EVEN, CMP_LEN, d), 0.1),
        'ev_pe_v': nrm(k[3], (N_EVEN, CMP_LEN, d), 0.1),
        'ev_w1_k': nrm(k[4], (N_EVEN, CMP_LEN * d, d), (CMP_LEN * d) ** -0.5),
        'ev_w2_k': nrm(k[5], (N_EVEN, d, d), d ** -0.5),
        'ev_w1_v': nrm(k[6], (N_EVEN, CMP_LEN * d, d), (CMP_LEN * d) ** -0.5),
        'ev_w2_v': nrm(k[7], (N_EVEN, d, d), d ** -0.5),
        'ev_w_out': nrm(k[8], (N_EVEN, EVEN_OUT, D_MODEL), EVEN_OUT ** -0.5 * DEEPNORM_BETA),
        'ev_ln_g': 1.0 + nrm(k[9], (N_EVEN, D_MODEL), 0.02),
        'ev_ln_b': nrm(k[10], (N_EVEN, D_MODEL), 0.02),
        'od_w_in': nrm(k[11], (N_ODD, D_MODEL, ODD_IN), D_MODEL ** -0.5),
        'od_lq1': nrm(k[12], (N_ODD, d), LAMBDA_STD),
        'od_lk1': nrm(k[13], (N_ODD, d), LAMBDA_STD),
        'od_lq2': nrm(k[14], (N_ODD, d), LAMBDA_STD),
        'od_lk2': nrm(k[15], (N_ODD, d), LAMBDA_STD),
        'od_gn_g': 1.0 + nrm(k[16], (N_ODD, DIFF_HEADS * 2 * d), 0.02),
        'od_w_out': nrm(k[17], (N_ODD, ODD_OUT, D_MODEL), ODD_OUT ** -0.5 * DEEPNORM_BETA),
        'od_ln_g': 1.0 + nrm(k[18], (N_ODD, D_MODEL), 0.02),
        'od_ln_b': nrm(k[19], (N_ODD, D_MODEL), 0.02),
    }


def reference(x, ev_w_in, ev_pe_k, ev_pe_v, ev_w1_k, ev_w2_k, ev_w1_v, ev_w2_v, ev_w_out, ev_ln_g, ev_ln_b,
              od_w_in, od_lq1, od_lk1, od_lq2, od_lk2, od_gn_g, od_w_out, od_ln_g, od_ln_b):
    cos, sin = _rope_tables(x.shape[1])
    for layer in range(DEPTH):
        i = layer // 2
        if layer % 2 == 0:
            x = _even_layer(x, ev_w_in[i], ev_pe_k[i], ev_pe_v[i], ev_w1_k[i], ev_w2_k[i], ev_w1_v[i],
                            ev_w2_v[i], ev_w_out[i], ev_ln_g[i], ev_ln_b[i], cos, sin)
        else:
            lambda_init = 0.8 - 0.6 * math.exp(-0.3 * layer)
            x = _odd_layer(x, od_w_in[i], od_lq1[i], od_lk1[i], od_lq2[i], od_lk2[i], od_gn_g[i],
                           od_w_out[i], od_ln_g[i], od_ln_b[i], lambda_init, cos, sin)
    return x
```

```python
import functools
import math

import jax
import jax.numpy as jnp
from jax import lax
from jax.experimental import pallas as pl
from jax.experimental.pallas import tpu as pltpu

F32 = jnp.float32
BF16 = jnp.bfloat16

HEAD_DIM = 128
ROPE_THETA = 10000.0
NSA_HEADS = 8
NSA_KV_HEADS = 2
NSA_GROUP = NSA_HEADS // NSA_KV_HEADS
CMP_LEN = 32
CMP_STRIDE = 16
SEL_LEN = 64
SEL_SHIFT = SEL_LEN.bit_length() - 1
SEL_TOPK = 16
WIN = 512
SEL_FORCE = 1.0e4
SB_HEADS = 8
DIFF_HEADS = 8
DEPTH = 2
DEEPNORM_ALPHA = (2.0 * DEPTH) ** 0.25
LN_EPS = 1e-5
RMS_EPS = 1e-5
SCALE = HEAD_DIM ** -0.5

LANES = 128
NEG = -1.0e30
VMEM_LIMIT = 56 * 1024 * 1024

EV_QA, EV_KC, EV_VC, EV_KS, EV_KW, EV_VS, EV_VW, EV_GA = 0, 8, 10, 12, 14, 16, 18, 20
EV_GATE_A, EV_QB, EV_KB, EV_VB, EV_GATE_B = 24, 32, 40, 48, 56
EV_WIDTH = 64 * LANES


def _cparams(sem):
    return pltpu.CompilerParams(dimension_semantics=sem, vmem_limit_bytes=VMEM_LIMIT)


def _dot_nt(a, b):
    return lax.dot_general(a, b, (((1,), (1,)), ((), ())), preferred_element_type=F32)


def _dot(a, b):
    return jnp.dot(a, b, preferred_element_type=F32)


def _silu(x):
    return x * (1.0 / (1.0 + jnp.exp(-x)))


def _sigmoid(x):
    return 1.0 / (1.0 + jnp.exp(-x))


def _rope_f32(x, cos2, sin2):
    return x * cos2 + pltpu.roll(x, HEAD_DIM // 2, axis=1) * sin2


def _inproj_kernel(x_ref, w_ref, cos_ref, sin_ref, o_ref, xb_ref, *, tile_modes):
    n = pl.program_id(1)

    @pl.when(n == 0)
    def _():
        xb_ref[...] = x_ref[...].astype(BF16)

    acc = _dot(xb_ref[...], w_ref[...])
    patterns = {}
    for tile, pat in enumerate(tile_modes):
        patterns.setdefault(pat, []).append(tile)
    for pat, tiles in patterns.items():
        cond = functools.reduce(jnp.logical_or, [n == t for t in tiles])

        @pl.when(cond)
        def _(pat=pat):
            for c, (rope, scale) in enumerate(pat):
                y = acc[:, c * LANES:(c + 1) * LANES]
                if rope:
                    y = _rope_f32(y, cos_ref[...], sin_ref[...])
                if scale:
                    y = y * SCALE
                o_ref[:, c * LANES:(c + 1) * LANES] = y.astype(o_ref.dtype)


def _inproj(x, w, cos2, sin2, tile_modes, tm, tn):
    m, k = x.shape
    n = w.shape[1]
    s = cos2.shape[0]
    assert m % tm == 0 and n % tn == 0 and s % tm == 0 and len(tile_modes) == n // tn
    pos_tiles = s // tm
    return pl.pallas_call(
        functools.partial(_inproj_kernel, tile_modes=tile_modes),
        grid=(m // tm, n // tn),
        in_specs=[
            pl.BlockSpec((tm, k), lambda i, j: (i, 0)),
            pl.BlockSpec((k, tn), lambda i, j: (0, j)),
            pl.BlockSpec((tm, LANES), lambda i, j: (i % pos_tiles, 0)),
            pl.BlockSpec((tm, LANES), lambda i, j: (i % pos_tiles, 0)),
        ],
        out_specs=pl.BlockSpec((tm, tn), lambda i, j: (i, j)),
        out_shape=jax.ShapeDtypeStruct((m, n), BF16),
        scratch_shapes=[pltpu.VMEM((tm, k), BF16)],
        compiler_params=_cparams(("arbitrary", "arbitrary")),
        name="inproj",
    )(x, w, cos2, sin2)


def _outproj_ln_kernel(*refs, n_parts):
    o_refs = refs[:n_parts]
    w_ref, x_ref, g_ref, b_ref, out_ref = refs[n_parts:]
    kp = w_ref.shape[0] // n_parts
    y = None
    for p in range(n_parts):
        part = _dot(o_refs[p][...], w_ref[p * kp:(p + 1) * kp, :])
        y = part if y is None else y + part
    z = DEEPNORM_ALPHA * x_ref[...] + y
    mu = jnp.mean(z, axis=-1, keepdims=True)
    zc = z - mu
    var = jnp.mean(zc * zc, axis=-1, keepdims=True)
    out_ref[...] = zc * lax.rsqrt(var + LN_EPS) * g_ref[...] + b_ref[...]


def _outproj_ln(parts, w, x, g, b, tm):
    m, d = x.shape
    n_parts = len(parts)
    kp = w.shape[0] // n_parts
    return pl.pallas_call(
        functools.partial(_outproj_ln_kernel, n_parts=n_parts),
        grid=(m // tm,),
        in_specs=[pl.BlockSpec((tm, kp), lambda i: (i, 0)) for _ in parts] + [
            pl.BlockSpec(w.shape, lambda i: (0, 0)),
            pl.BlockSpec((tm, d), lambda i: (i, 0)),
            pl.BlockSpec((1, d), lambda i: (0, 0)),
            pl.BlockSpec((1, d), lambda i: (0, 0)),
        ],
        out_specs=pl.BlockSpec((tm, d), lambda i: (i, 0)),
        out_shape=jax.ShapeDtypeStruct((m, d), F32),
        compiler_params=_cparams(("arbitrary",)),
        name="outproj_ln",
    )(*parts, w, x, g.reshape(1, d), b.reshape(1, d))


def _compress_kernel(kv_ref, pe_ref, w1_ref, w2_ref, o_ref, pad_ref, *, seq):
    n_rows = seq // CMP_STRIDE
    pad_ref[0:seq, :] = kv_ref[...].astype(F32)
    pad_ref[seq:seq + CMP_STRIDE, :] = jnp.zeros((CMP_STRIDE, HEAD_DIM), F32)
    pre = jnp.zeros((n_rows, HEAD_DIM), F32)
    for l in range(CMP_LEN):
        rows = pad_ref[pl.ds(l, n_rows, stride=CMP_STRIDE), :] + pe_ref[l:l + 1, :]
        pre = pre + _dot(rows.astype(BF16), w1_ref[l])
    o_ref[...] = _dot(_silu(pre).astype(BF16), w2_ref[...]).astype(o_ref.dtype)


def _compress(h, pe, w1, w2, batch, seq):
    n_rows = seq // CMP_STRIDE
    hk = NSA_KV_HEADS
    return pl.pallas_call(
        functools.partial(_compress_kernel, seq=seq),
        grid=(2, batch, hk),
        in_specs=[
            pl.BlockSpec((seq, HEAD_DIM), lambda c, b, h: (b, EV_KC + c * hk + h)),
            pl.BlockSpec((None, CMP_LEN, HEAD_DIM), lambda c, b, h: (c, 0, 0)),
            pl.BlockSpec((None, CMP_LEN, HEAD_DIM, HEAD_DIM), lambda c, b, h: (c, 0, 0, 0)),
            pl.BlockSpec((None, HEAD_DIM, HEAD_DIM), lambda c, b, h: (c, 0, 0)),
        ],
        out_specs=pl.BlockSpec((None, None, n_rows, HEAD_DIM), lambda c, b, h: (c, b * hk + h, 0, 0)),
        out_shape=jax.ShapeDtypeStruct((2, batch * hk, n_rows, HEAD_DIM), BF16),
        scratch_shapes=[pltpu.VMEM((seq + CMP_STRIDE, HEAD_DIM), F32)],
        compiler_params=_cparams(("arbitrary", "arbitrary", "arbitrary")),
        name="nsa_compress",
    )(h, pe, w1, w2)


def _heads_to_rows(x, g):
    return jnp.concatenate([x[:, i * HEAD_DIM:(i + 1) * HEAD_DIM] for i in range(g)], axis=0)


def _rows_to_heads(x, g):
    r = x.shape[0] // g
    return jnp.concatenate([x[i * r:(i + 1) * r, :] for i in range(g)], axis=1)


def _cmp_topk_kernel(q_ref, kc_ref, vc_ref, ocmp_ref, sel_ref, *, tq, n_sel_blk, n_sel):
    g = NSA_GROUP
    i = pl.program_id(2)
    q0 = i * tq
    n_rows = kc_ref.shape[0]
    q4 = _heads_to_rows(q_ref[...], g)
    sc = _dot_nt(q4, kc_ref[...]).reshape(g, tq, n_rows)
    t_col = q0 + lax.broadcasted_iota(jnp.int32, (tq, n_rows), 0)
    blk_end = lax.broadcasted_iota(jnp.int32, (tq, n_rows), 1) * CMP_STRIDE + (CMP_LEN - 1)
    vis = (blk_end <= t_col)[None]
    sm = jnp.where(vis, sc, NEG)
    mx = jnp.max(sm, axis=-1, keepdims=True)
    e = jnp.where(vis, jnp.exp(sm - mx), 0.0)
    den = jnp.sum(e, axis=-1, keepdims=True)
    p = e / jnp.where(den > 0.0, den, 1.0)
    o = _dot(p.reshape(g * tq, n_rows).astype(BF16), vc_ref[...])
    ocmp_ref[...] = _rows_to_heads(o, g).astype(ocmp_ref.dtype)

    p_sum = jnp.sum(p, axis=0)
    jj = lax.broadcasted_iota(jnp.int32, (n_sel_blk, n_rows), 0) * SEL_LEN
    cs = lax.broadcasted_iota(jnp.int32, (n_sel_blk, n_rows), 1) * CMP_STRIDE
    ov_t = jnp.where((cs < jj + SEL_LEN) & (cs + CMP_LEN > jj), 1.0, 0.0).astype(BF16)
    p_hi = p_sum.astype(BF16)
    p_lo = (p_sum - p_hi.astype(F32)).astype(BF16)
    imp_t = _dot_nt(ov_t, p_hi) + _dot_nt(ov_t, p_lo)

    blk = lax.broadcasted_iota(jnp.int32, (n_sel_blk, tq), 0)
    j_cur = (q0 + lax.broadcasted_iota(jnp.int32, (n_sel_blk, tq), 1)) >> SEL_SHIFT
    forced = (blk == 0) | (blk == j_cur) | (blk == j_cur - 1)
    score = jnp.where(forced, SEL_FORCE, jnp.where(blk <= j_cur, imp_t, -SEL_FORCE))
    rank = jnp.zeros((n_sel_blk, tq), F32)
    for r in range(n_sel_blk):
        row = score[r:r + 1, :]
        ahead = (row > score) | ((row == score) & (blk > r))
        rank = rank + jnp.where(ahead, 1.0, 0.0)
    sel_t = jnp.where(rank < float(n_sel), 1.0, 0.0)
    if n_sel_blk < LANES:
        sel_t = jnp.concatenate([sel_t, jnp.zeros((LANES - n_sel_blk, tq), F32)], axis=0)
    sel_ref[...] = sel_t.T.astype(sel_ref.dtype)


def _cmp_topk(h, cmp_kv, batch, seq, tq):
    hk, g = NSA_KV_HEADS, NSA_GROUP
    nq = seq // tq
    n_rows = seq // CMP_STRIDE
    n_sel_blk = seq // SEL_LEN
    assert n_sel_blk <= LANES and tq == LANES
    n_sel = min(SEL_TOPK, n_sel_blk)
    return pl.pallas_call(
        functools.partial(_cmp_topk_kernel, tq=tq, n_sel_blk=n_sel_blk, n_sel=n_sel),
        grid=(batch, hk, nq),
        in_specs=[
            pl.BlockSpec((tq, g * HEAD_DIM), lambda b, h, i: (b * nq + i, EV_QA // g + h)),
            pl.BlockSpec((None, None, n_rows, HEAD_DIM), lambda b, h, i: (0, b * hk + h, 0, 0)),
            pl.BlockSpec((None, None, n_rows, HEAD_DIM), lambda b, h, i: (1, b * hk + h, 0, 0)),
        ],
        out_specs=[
            pl.BlockSpec((tq, g * HEAD_DIM), lambda b, h, i: (b * nq + i, h)),
            pl.BlockSpec((tq, LANES), lambda b, h, i: ((b * hk + h) * nq + i, 0)),
        ],
        out_shape=[
            jax.ShapeDtypeStruct((batch * seq, NSA_HEADS * HEAD_DIM), BF16),
            jax.ShapeDtypeStruct((batch * hk * seq, LANES), BF16),
        ],
        compiler_params=_cparams(("arbitrary", "arbitrary", "arbitrary")),
        name="nsa_cmp_topk",
    )(h, cmp_kv, cmp_kv)


def _softmax_step(s, valid, m, l, acc, v):
    g, tq, tk = s.shape
    if valid is not None:
        s = jnp.where(valid[None], s, NEG)
    m_new = jnp.maximum(m, jnp.max(s, axis=-1, keepdims=True))
    alpha = jnp.exp(m - m_new)
    p = jnp.exp(s - m_new)
    l = alpha * l + jnp.sum(p, axis=-1, keepdims=True)
    pv = _dot(p.reshape(g * tq, tk).astype(BF16), v).reshape(g, tq, v.shape[-1])
    return m_new, l, alpha * acc + pv


def _nsa_attn_kernel(q_ref, sel_ref, ocmp_ref, ks_ref, vs_ref, kw_ref, vw_ref, cos_ref, sin_ref, ga_ref,
                     gate_ref, o_ref, *, tq):
    g = NSA_GROUP
    tk = tq
    i = pl.program_id(2)
    q0 = i * tq
    cos2 = jnp.concatenate([cos_ref[...]] * g, axis=0)
    sin2 = jnp.concatenate([sin_ref[...]] * g, axis=0)
    q4 = _rope_f32(_heads_to_rows(q_ref[...], g).astype(F32), cos2, sin2).astype(BF16)
    t_row = q0 + lax.broadcasted_iota(jnp.int32, (tq, tk), 0)
    k_lane = lax.broadcasted_iota(jnp.int32, (tq, tk), 1)
    sel = sel_ref[...]
    blk_row = lax.broadcasted_iota(jnp.int32, (LANES, tk), 0)
    blk_of_lane = lax.broadcasted_iota(jnp.int32, (LANES, tk), 1) >> SEL_SHIFT
    init = (jnp.full((g, tq, 1), NEG, F32), jnp.zeros((g, tq, 1), F32), jnp.zeros((g, tq, HEAD_DIM), F32))

    def sel_body(j, carry):
        k0 = pl.multiple_of(j * tk, tk)
        s = _dot_nt(q4, ks_ref[pl.ds(k0, tk), :]).reshape(g, tq, tk)
        expand = jnp.where(blk_row == j * (tk // SEL_LEN) + blk_of_lane, 1.0, 0.0).astype(BF16)
        picked = _dot(sel, expand)
        valid = (picked > 0.5) & (k0 + k_lane <= t_row)
        return _softmax_step(s, valid, *carry, vs_ref[pl.ds(k0, tk), :])

    m_s, l_s, acc_s = lax.fori_loop(0, i + 1, sel_body, init)
    o_sel = acc_s / l_s

    def win_body(jj, carry):
        k0 = pl.multiple_of((i - jj) * tk, tk)
        s = _dot_nt(q4, kw_ref[pl.ds(k0, tk), :]).reshape(g, tq, tk)
        dist = t_row - (k0 + k_lane)
        valid = (dist >= 0) & (dist < WIN)
        return _softmax_step(s, valid, *carry, vw_ref[pl.ds(k0, tk), :])

    m_w, l_w, acc_w = lax.fori_loop(0, jnp.minimum(i, WIN // tk) + 1, win_body, init)
    o_win = acc_w / l_w

    gt = _sigmoid(ga_ref[...].astype(F32))
    o_cmp = ocmp_ref[...].astype(F32)
    gate = _silu(gate_ref[...].astype(F32))
    for a in range(g):
        mix = (gt[:, 3 * a:3 * a + 1] * o_cmp[:, a * HEAD_DIM:(a + 1) * HEAD_DIM]
               + gt[:, 3 * a + 1:3 * a + 2] * o_sel[a] + gt[:, 3 * a + 2:3 * a + 3] * o_win[a])
        o_ref[:, a * HEAD_DIM:(a + 1) * HEAD_DIM] = (
            mix * gate[:, a * HEAD_DIM:(a + 1) * HEAD_DIM]).astype(o_ref.dtype)


def _nsa_attn(h, sel, ocmp, cos2, sin2, batch, seq, tq):
    hk, g = NSA_KV_HEADS, NSA_GROUP
    nq = seq // tq
    assert WIN % tq == 0 and tq % SEL_LEN == 0
    slab = lambda col: pl.BlockSpec((seq, HEAD_DIM), lambda b, h_, i: (b, col + h_))
    return pl.pallas_call(
        functools.partial(_nsa_attn_kernel, tq=tq),
        grid=(batch, hk, nq),
        in_specs=[
            pl.BlockSpec((tq, g * HEAD_DIM), lambda b, h_, i: (b * nq + i, EV_QA // g + h_)),
            pl.BlockSpec((tq, LANES), lambda b, h_, i: ((b * hk + h_) * nq + i, 0)),
            pl.BlockSpec((tq, g * HEAD_DIM), lambda b, h_, i: (b * nq + i, h_)),
            slab(EV_KS), slab(EV_VS), slab(EV_KW), slab(EV_VW),
            pl.BlockSpec((tq, LANES), lambda b, h_, i: (i, 0)),
            pl.BlockSpec((tq, LANES), lambda b, h_, i: (i, 0)),
            pl.BlockSpec((tq, LANES), lambda b, h_, i: (b * nq + i, EV_GA + h_)),
            pl.BlockSpec((tq, g * HEAD_DIM), lambda b, h_, i: (b * nq + i, EV_GATE_A // g + h_)),
        ],
        out_specs=pl.BlockSpec((tq, g * HEAD_DIM), lambda b, h_, i: (b * nq + i, h_)),
        out_shape=jax.ShapeDtypeStruct((batch * seq, NSA_HEADS * HEAD_DIM), BF16),
        compiler_params=_cparams(("arbitrary", "arbitrary", "arbitrary")),
        name="nsa_attn",
    )(h, sel, ocmp, h, h, h, h, cos2, sin2, h, h)


def _sb_kernel(q_ref, k_ref, v_ref, gate_ref, tri_ref, o_ref, *, t):
    i = pl.program_id(2)
    q = q_ref[...]
    strict = lax.broadcasted_iota(jnp.int32, (t, t), 1) < lax.broadcasted_iota(jnp.int32, (t, t), 0)
    tri = tri_ref[...]

    def tile(j, later, acc, mask):
        k0 = pl.multiple_of(j * t, t)
        z = _dot_nt(q, k_ref[pl.ds(k0, t), :])
        sp = jnp.maximum(z, 0.0) + jnp.log(1.0 + jnp.exp(-jnp.abs(z)))
        log1m = -sp if mask is None else jnp.where(mask, -sp, 0.0)
        hi = log1m.astype(BF16)
        lo = (log1m - hi.astype(F32)).astype(BF16)
        after = _dot(hi, tri) + _dot(lo, tri)
        a = jnp.exp(z - sp + after + later)
        if mask is not None:
            a = jnp.where(mask, a, 0.0)
        acc = acc + _dot(a.astype(BF16), v_ref[pl.ds(k0, t), :])
        later = later + after[:, 0:1] + log1m[:, 0:1]
        return later, acc

    later, acc = tile(i, jnp.zeros((t, 1), F32), jnp.zeros((t, HEAD_DIM), F32), strict)
    later, acc = lax.fori_loop(0, i, lambda jj, c: tile(i - 1 - jj, c[0], c[1], None), (later, acc))
    o_ref[...] = (acc * _silu(gate_ref[...].astype(F32))).astype(o_ref.dtype)


def _stick_breaking(h, batch, seq, t):
    nh = SB_HEADS
    nq = seq // t
    tri = (lax.broadcasted_iota(jnp.int32, (t, t), 0) > lax.broadcasted_iota(jnp.int32, (t, t), 1)).astype(BF16)
    return pl.pallas_call(
        functools.partial(_sb_kernel, t=t),
        grid=(batch, nh, nq),
        in_specs=[
            pl.BlockSpec((t, HEAD_DIM), lambda b, h_, i: (b * nq + i, EV_QB + h_)),
            pl.BlockSpec((seq, HEAD_DIM), lambda b, h_, i: (b, EV_KB + h_)),
            pl.BlockSpec((seq, HEAD_DIM), lambda b, h_, i: (b, EV_VB + h_)),
            pl.BlockSpec((t, HEAD_DIM), lambda b, h_, i: (b * nq + i, EV_GATE_B + h_)),
            pl.BlockSpec((t, t), lambda b, h_, i: (0, 0)),
        ],
        out_specs=pl.BlockSpec((t, HEAD_DIM), lambda b, h_, i: (b * nq + i, h_)),
        out_shape=jax.ShapeDtypeStruct((batch * seq, nh * HEAD_DIM), BF16),
        compiler_params=_cparams(("arbitrary", "arbitrary", "arbitrary")),
        name="stick_breaking",
    )(h, h, h, h, tri)


def _diff_kernel(q_ref, k_ref, v_ref, gate_ref, gn_ref, lq1_ref, lk1_ref, lq2_ref, lk2_ref, o_ref, *,
                 t, lambda_init):
    d = HEAD_DIM
    i = pl.program_id(2)
    q = q_ref[...]
    t_row = lax.broadcasted_iota(jnp.int32, (t, t), 0)
    k_lane = lax.broadcasted_iota(jnp.int32, (t, t), 1)
    causal = k_lane <= t_row

    def tile(j, carry, mask):
        k0 = pl.multiple_of(j * t, t)
        kk = k_ref[pl.ds(k0, t), :]
        vv = v_ref[pl.ds(k0, t), :]
        out = []
        for c in range(2):
            s = _dot_nt(q[:, c * d:(c + 1) * d], kk[:, c * d:(c + 1) * d])[None]
            m, l, acc = _softmax_step(s, mask, *carry[c], vv)
            out.append((m, l, acc))
        return tuple(out)

    one = (jnp.full((1, t, 1), NEG, F32), jnp.zeros((1, t, 1), F32), jnp.zeros((1, t, 2 * d), F32))
    carry = tile(i, (one, one), causal)
    carry = lax.fori_loop(0, i, lambda j, c: tile(j, c, None), carry)
    (_, l0, a0), (_, l1, a1) = carry
    lam = (jnp.exp(jnp.sum(lq1_ref[...] * lk1_ref[...], axis=-1, keepdims=True))
           - jnp.exp(jnp.sum(lq2_ref[...] * lk2_ref[...], axis=-1, keepdims=True)) + lambda_init)
    o = a0[0] / l0[0] - lam * (a1[0] / l1[0])
    o = o * lax.rsqrt(jnp.mean(o * o, axis=-1, keepdims=True) + RMS_EPS)
    o = o * gn_ref[...] * (1.0 - lambda_init)
    o_ref[...] = (o * _silu(gate_ref[...].astype(F32))).astype(o_ref.dtype)


def _diff_attn(h, gn_g, lq1, lk1, lq2, lk2, lambda_init, batch, seq, t):
    nh = DIFF_HEADS
    nq = seq // t
    w = 2 * HEAD_DIM
    vec = pl.BlockSpec((1, HEAD_DIM), lambda b, h_, i: (0, 0))
    r = lambda a: a.reshape(1, HEAD_DIM)
    return pl.pallas_call(
        functools.partial(_diff_kernel, t=t, lambda_init=lambda_init),
        grid=(batch, nh, nq),
        in_specs=[
            pl.BlockSpec((t, w), lambda b, h_, i: (b * nq + i, h_)),
            pl.BlockSpec((seq, w), lambda b, h_, i: (b, nh + h_)),
            pl.BlockSpec((seq, w), lambda b, h_, i: (b, 2 * nh + h_)),
            pl.BlockSpec((t, w), lambda b, h_, i: (b * nq + i, 3 * nh + h_)),
            pl.BlockSpec((None, 1, w), lambda b, h_, i: (h_, 0, 0)),
            vec, vec, vec, vec,
        ],
        out_specs=pl.BlockSpec((t, w), lambda b, h_, i: (b * nq + i, h_)),
        out_shape=jax.ShapeDtypeStruct((batch * seq, nh * w), BF16),
        compiler_params=_cparams(("arbitrary", "arbitrary", "arbitrary")),
        name="diff_attn",
    )(h, h, h, h, gn_g.reshape(nh, 1, w), r(lq1), r(lk1), r(lq2), r(lk2))


def _rope_tables(seq):
    pos = jnp.arange(seq, dtype=F32)
    inv = ROPE_THETA ** (-jnp.arange(0, HEAD_DIM, 2, dtype=F32) / HEAD_DIM)
    ang = pos[:, None] * inv[None, :]
    cos, sin = jnp.cos(ang), jnp.sin(ang)
    return jnp.concatenate([cos, cos], axis=-1), jnp.concatenate([-sin, sin], axis=-1)


def _even_weight(w_in):
    hk, g, d = NSA_KV_HEADS, NSA_GROUP, HEAD_DIM
    a_q, a_kv, b_w = NSA_HEADS * d, hk * d, SB_HEADS * d
    sizes = (a_q, a_kv, a_kv, a_kv, a_kv, a_kv, a_kv, 3 * NSA_HEADS, a_q, b_w, b_w, b_w, b_w)
    offs = [0]
    for sz in sizes:
        offs.append(offs[-1] + sz)
    col = lambda idx: w_in[:, offs[idx]:offs[idx + 1]]
    qa, kc, vc, ks, vs, kw, vw, ga, gate_a, qb, kb, vb, gate_b = [col(i) for i in range(13)]
    zeros = lambda n: jnp.zeros((w_in.shape[0], n), w_in.dtype)
    ga_blocks = []
    for h in range(hk):
        ga_blocks += [ga[:, h * 3 * g:(h + 1) * 3 * g], zeros(LANES - 3 * g)]
    ga_blocks.append(zeros((EV_GATE_A - EV_GA - hk) * LANES))
    w = jnp.concatenate([qa, kc, vc, ks, kw, vs, vw] + ga_blocks + [gate_a, qb, kb, vb, gate_b], axis=1)
    assert w.shape[1] == EV_WIDTH
    return w.astype(BF16)


def _even_tile_modes(tn):
    modes = [(False, False)] * (EV_WIDTH // LANES)
    for c in range(EV_QA, EV_QA + NSA_HEADS):
        modes[c] = (False, True)
    for c in range(EV_KS, EV_KS + 2 * NSA_KV_HEADS):
        modes[c] = (True, False)
    for c in range(EV_QB, EV_QB + SB_HEADS):
        modes[c] = (False, True)
    per = tn // LANES
    return tuple(tuple(modes[i * per:(i + 1) * per]) for i in range(len(modes) // per))


def _odd_tile_modes(tn):
    nh = 2 * DIFF_HEADS
    modes = [(True, True)] * nh + [(True, False)] * nh + [(False, False)] * (2 * nh)
    per = tn // LANES
    return tuple(tuple(modes[i * per:(i + 1) * per]) for i in range(len(modes) // per))


TM_PROJ, TN_PROJ = 1024, 1024
TM_OUT = 256
TQ_NSA = 128
T_SB = 256
T_DIFF = 256


def _even_layer(x2, batch, seq, w_in, pe_k, pe_v, w1k, w2k, w1v, w2v, w_out, ln_g, ln_b, cos2, sin2):
    d = HEAD_DIM
    tm = min(TM_PROJ, seq)
    h = _inproj(x2, _even_weight(w_in), cos2, sin2, _even_tile_modes(TN_PROJ), tm, TN_PROJ)
    pe = jnp.stack([pe_k, pe_v])
    w1 = jnp.stack([w1k, w1v]).reshape(2, CMP_LEN, d, d).astype(BF16)
    w2 = jnp.stack([w2k, w2v]).astype(BF16)
    cmp_kv = _compress(h, pe, w1, w2, batch, seq)
    ocmp, sel = _cmp_topk(h, cmp_kv, batch, seq, TQ_NSA)
    o_a = _nsa_attn(h, sel, ocmp, cos2, sin2, batch, seq, TQ_NSA)
    o_b = _stick_breaking(h, batch, seq, min(T_SB, seq))
    return _outproj_ln([o_a, o_b], w_out.astype(BF16), x2, ln_g, ln_b, TM_OUT)


def _odd_layer(x2, batch, seq, w_in, lq1, lk1, lq2, lk2, gn_g, w_out, ln_g, ln_b, lambda_init, cos2, sin2):
    tm = min(TM_PROJ, seq)
    h = _inproj(x2, w_in.astype(BF16), cos2, sin2, _odd_tile_modes(TN_PROJ), tm, TN_PROJ)
    o = _diff_attn(h, gn_g, lq1, lk1, lq2, lk2, lambda_init, batch, seq, min(T_DIFF, seq))
    return _outproj_ln([o], w_out.astype(BF16), x2, ln_g, ln_b, TM_OUT)


def kernel(x, ev_w_in, ev_pe_k, ev_pe_v, ev_w1_k, ev_w2_k, ev_w1_v, ev_w2_v, ev_w_out, ev_ln_g, ev_ln_b,
           od_w_in, od_lq1, od_lk1, od_lq2, od_lk2, od_gn_g, od_w_out, od_ln_g, od_ln_b):
    batch, seq, dm = x.shape
    cos2, sin2 = _rope_tables(seq)
    x2 = x.reshape(batch * seq, dm)
    for layer in range(DEPTH):
        i = layer // 2
        if layer % 2 == 0:
            x2 = _even_layer(x2, batch, seq, ev_w_in[i], ev_pe_k[i], ev_pe_v[i], ev_w1_k[i], ev_w2_k[i],
                             ev_w1_v[i], ev_w2_v[i], ev_w_out[i], ev_ln_g[i], ev_ln_b[i], cos2, sin2)
        else:
            lambda_init = 0.8 - 0.6 * math.exp(-0.3 * layer)
            x2 = _odd_layer(x2, batch, seq, od_w_in[i], od_lq1[i], od_lk1[i], od_lq2[i], od_lk2[i], od_gn_g[i],
                            od_w_out[i], od_ln_g[i], od_ln_b[i], lambda_init, cos2, sin2)
    return x2.reshape(batch, seq, dm)
```

```python
import functools
import math

import jax
import jax.numpy as jnp
from jax import lax
from jax.experimental import pallas as pl
from jax.experimental.pallas import tpu as pltpu

F32 = jnp.float32
BF16 = jnp.bfloat16

HEAD_DIM = 128
ROPE_THETA = 10000.0
NSA_HEADS = 8
NSA_KV_HEADS = 2
NSA_GROUP = NSA_HEADS // NSA_KV_HEADS
CMP_LEN = 32
CMP_STRIDE = 16
SEL_LEN = 64
SEL_SHIFT = SEL_LEN.bit_length() - 1
SEL_TOPK = 16
WIN = 512
SEL_FORCE = 1.0e4
SB_HEADS = 8
DIFF_HEADS = 8
DEPTH = 2
DEEPNORM_ALPHA = (2.0 * DEPTH) ** 0.25
LN_EPS = 1e-5
RMS_EPS = 1e-5
LOG2E = math.log2(math.e)
SCALE = HEAD_DIM ** -0.5 * LOG2E

LANES = 128
NEG = -1.0e30
VMEM_LIMIT = 56 * 1024 * 1024

EV_QA, EV_KC, EV_VC, EV_KS, EV_KW, EV_VS, EV_VW, EV_GA = 0, 8, 10, 12, 14, 16, 18, 20
EV_GATE_A, EV_QB, EV_KB, EV_VB, EV_GATE_B = 24, 32, 40, 48, 56
EV_WIDTH = 64 * LANES


def _cparams(sem):
    return pltpu.CompilerParams(dimension_semantics=sem, vmem_limit_bytes=VMEM_LIMIT)


def _dot_nt(a, b):
    return lax.dot_general(a, b, (((1,), (1,)), ((), ())), preferred_element_type=F32)


def _dot(a, b):
    return jnp.dot(a, b, preferred_element_type=F32)


def _silu(x):
    return x * (1.0 / (1.0 + jnp.exp(-x)))


def _sigmoid(x):
    return 1.0 / (1.0 + jnp.exp(-x))


def _rope_f32(x, cos2, sin2):
    return x * cos2 + pltpu.roll(x, HEAD_DIM // 2, axis=1) * sin2


def _inproj_kernel(x_ref, w_ref, cos_ref, sin_ref, o_ref, xb_ref, *, tile_modes):
    n = pl.program_id(1)

    @pl.when(n == 0)
    def _():
        xb_ref[...] = x_ref[...].astype(BF16)

    acc = _dot(xb_ref[...], w_ref[...])
    patterns = {}
    for tile, pat in enumerate(tile_modes):
        patterns.setdefault(pat, []).append(tile)
    for pat, tiles in patterns.items():
        cond = functools.reduce(jnp.logical_or, [n == t for t in tiles])

        @pl.when(cond)
        def _(pat=pat):
            for c, (rope, scale) in enumerate(pat):
                y = acc[:, c * LANES:(c + 1) * LANES]
                if rope:
                    y = _rope_f32(y, cos_ref[...], sin_ref[...])
                if scale:
                    y = y * SCALE
                o_ref[:, c * LANES:(c + 1) * LANES] = y.astype(o_ref.dtype)


def _inproj(x, w, cos2, sin2, tile_modes, tm, tn):
    m, k = x.shape
    n = w.shape[1]
    s = cos2.shape[0]
    assert m % tm == 0 and n % tn == 0 and s % tm == 0 and len(tile_modes) == n // tn
    pos_tiles = s // tm
    return pl.pallas_call(
        functools.partial(_inproj_kernel, tile_modes=tile_modes),
        grid=(m // tm, n // tn),
        in_specs=[
            pl.BlockSpec((tm, k), lambda i, j: (i, 0)),
            pl.BlockSpec((k, tn), lambda i, j: (0, j)),
            pl.BlockSpec((tm, LANES), lambda i, j: (i % pos_tiles, 0)),
            pl.BlockSpec((tm, LANES), lambda i, j: (i % pos_tiles, 0)),
        ],
        out_specs=pl.BlockSpec((tm, tn), lambda i, j: (i, j)),
        out_shape=jax.ShapeDtypeStruct((m, n), BF16),
        scratch_shapes=[pltpu.VMEM((tm, k), BF16)],
        compiler_params=_cparams(("arbitrary", "arbitrary")),
        name="inproj",
    )(x, w, cos2, sin2)


def _outproj_ln_kernel(*refs, n_parts):
    o_refs = refs[:n_parts]
    w_ref, x_ref, g_ref, b_ref, out_ref = refs[n_parts:]
    kp = w_ref.shape[0] // n_parts
    y = None
    for p in range(n_parts):
        part = _dot(o_refs[p][...], w_ref[p * kp:(p + 1) * kp, :])
        y = part if y is None else y + part
    z = DEEPNORM_ALPHA * x_ref[...] + y
    mu = jnp.mean(z, axis=-1, keepdims=True)
    zc = z - mu
    var = jnp.mean(zc * zc, axis=-1, keepdims=True)
    out_ref[...] = zc * lax.rsqrt(var + LN_EPS) * g_ref[...] + b_ref[...]


def _outproj_ln(parts, w, x, g, b, tm):
    m, d = x.shape
    n_parts = len(parts)
    kp = w.shape[0] // n_parts
    return pl.pallas_call(
        functools.partial(_outproj_ln_kernel, n_parts=n_parts),
        grid=(m // tm,),
        in_specs=[pl.BlockSpec((tm, kp), lambda i: (i, 0)) for _ in parts] + [
            pl.BlockSpec(w.shape, lambda i: (0, 0)),
            pl.BlockSpec((tm, d), lambda i: (i, 0)),
            pl.BlockSpec((1, d), lambda i: (0, 0)),
            pl.BlockSpec((1, d), lambda i: (0, 0)),
        ],
        out_specs=pl.BlockSpec((tm, d), lambda i: (i, 0)),
        out_shape=jax.ShapeDtypeStruct((m, d), F32),
        compiler_params=_cparams(("arbitrary",)),
        name="outproj_ln",
    )(*parts, w, x, g.reshape(1, d), b.reshape(1, d))


def _compress_kernel(kv_ref, pe_ref, w1_ref, w2_ref, o_ref, pad_ref, *, seq):
    n_rows = seq // CMP_STRIDE
    pad_ref[0:seq, :] = kv_ref[...].astype(F32)
    pad_ref[seq:seq + CMP_STRIDE, :] = jnp.zeros((CMP_STRIDE, HEAD_DIM), F32)
    pre = jnp.zeros((n_rows, HEAD_DIM), F32)
    for l in range(CMP_LEN):
        rows = pad_ref[pl.ds(l, n_rows, stride=CMP_STRIDE), :] + pe_ref[l:l + 1, :]
        pre = pre + _dot(rows.astype(BF16), w1_ref[l])
    o_ref[...] = _dot(_silu(pre).astype(BF16), w2_ref[...]).astype(o_ref.dtype)


def _compress(h, pe, w1, w2, batch, seq):
    n_rows = seq // CMP_STRIDE
    hk = NSA_KV_HEADS
    return pl.pallas_call(
        functools.partial(_compress_kernel, seq=seq),
        grid=(2, batch, hk),
        in_specs=[
            pl.BlockSpec((seq, HEAD_DIM), lambda c, b, h: (b, EV_KC + c * hk + h)),
            pl.BlockSpec((None, CMP_LEN, HEAD_DIM), lambda c, b, h: (c, 0, 0)),
            pl.BlockSpec((None, CMP_LEN, HEAD_DIM, HEAD_DIM), lambda c, b, h: (c, 0, 0, 0)),
            pl.BlockSpec((None, HEAD_DIM, HEAD_DIM), lambda c, b, h: (c, 0, 0)),
        ],
        out_specs=pl.BlockSpec((None, None, n_rows, HEAD_DIM), lambda c, b, h: (c, b * hk + h, 0, 0)),
        out_shape=jax.ShapeDtypeStruct((2, batch * hk, n_rows, HEAD_DIM), BF16),
        scratch_shapes=[pltpu.VMEM((seq + CMP_STRIDE, HEAD_DIM), F32)],
        compiler_params=_cparams(("arbitrary", "arbitrary", "arbitrary")),
        name="nsa_compress",
    )(h, pe, w1, w2)


def _heads_to_rows(x, g):
    return jnp.concatenate([x[:, i * HEAD_DIM:(i + 1) * HEAD_DIM] for i in range(g)], axis=0)


def _rows_to_heads(x, g):
    r = x.shape[0] // g
    return jnp.concatenate([x[i * r:(i + 1) * r, :] for i in range(g)], axis=1)


def _cmp_topk_kernel(q_ref, kc_ref, vc_ref, ocmp_ref, sel_ref, *, tq, n_sel_blk, n_sel):
    g = NSA_GROUP
    i = pl.program_id(2)
    q0 = i * tq
    n_rows = kc_ref.shape[0]
    q4 = _heads_to_rows(q_ref[...], g)
    sc = _dot_nt(q4, kc_ref[...]).reshape(g, tq, n_rows)
    t_col = q0 + lax.broadcasted_iota(jnp.int32, (tq, n_rows), 0)
    blk_end = lax.broadcasted_iota(jnp.int32, (tq, n_rows), 1) * CMP_STRIDE + (CMP_LEN - 1)
    vis = (blk_end <= t_col)[None]
    sm = jnp.where(vis, sc, NEG)
    mx = jnp.max(sm, axis=-1, keepdims=True)
    e = jnp.where(vis, jnp.exp2(sm - mx), 0.0)
    den = jnp.sum(e, axis=-1, keepdims=True)
    p = e / jnp.where(den > 0.0, den, 1.0)
    o = _dot(p.reshape(g * tq, n_rows).astype(BF16), vc_ref[...])
    ocmp_ref[...] = _rows_to_heads(o, g).astype(ocmp_ref.dtype)

    p_sum = jnp.sum(p, axis=0)
    jj = lax.broadcasted_iota(jnp.int32, (n_sel_blk, n_rows), 0) * SEL_LEN
    cs = lax.broadcasted_iota(jnp.int32, (n_sel_blk, n_rows), 1) * CMP_STRIDE
    ov_t = jnp.where((cs < jj + SEL_LEN) & (cs + CMP_LEN > jj), 1.0, 0.0).astype(BF16)
    p_hi = p_sum.astype(BF16)
    p_lo = (p_sum - p_hi.astype(F32)).astype(BF16)
    imp_t = _dot_nt(ov_t, p_hi) + _dot_nt(ov_t, p_lo)

    blk = lax.broadcasted_iota(jnp.int32, (n_sel_blk, tq), 0)
    j_cur = (q0 + lax.broadcasted_iota(jnp.int32, (n_sel_blk, tq), 1)) >> SEL_SHIFT
    forced = (blk == 0) | (blk == j_cur) | (blk == j_cur - 1)
    score = jnp.where(forced, SEL_FORCE, jnp.where(blk <= j_cur, imp_t, -SEL_FORCE))
    rank = jnp.zeros((n_sel_blk, tq), F32)
    for r in range(n_sel_blk):
        row = score[r:r + 1, :]
        ahead = (row > score) | ((row == score) & (blk > r))
        rank = rank + jnp.where(ahead, 1.0, 0.0)
    sel_t = jnp.where(rank < float(n_sel), 1.0, 0.0)
    if n_sel_blk < LANES:
        sel_t = jnp.concatenate([sel_t, jnp.zeros((LANES - n_sel_blk, tq), F32)], axis=0)
    sel_ref[...] = sel_t.T.astype(sel_ref.dtype)


def _cmp_topk(h, cmp_kv, batch, seq, tq):
    hk, g = NSA_KV_HEADS, NSA_GROUP
    nq = seq // tq
    n_rows = seq // CMP_STRIDE
    n_sel_blk = seq // SEL_LEN
    assert n_sel_blk <= LANES and tq == LANES
    n_sel = min(SEL_TOPK, n_sel_blk)
    return pl.pallas_call(
        functools.partial(_cmp_topk_kernel, tq=tq, n_sel_blk=n_sel_blk, n_sel=n_sel),
        grid=(batch, hk, nq),
        in_specs=[
            pl.BlockSpec((tq, g * HEAD_DIM), lambda b, h, i: (b * nq + i, EV_QA // g + h)),
            pl.BlockSpec((None, None, n_rows, HEAD_DIM), lambda b, h, i: (0, b * hk + h, 0, 0)),
            pl.BlockSpec((None, None, n_rows, HEAD_DIM), lambda b, h, i: (1, b * hk + h, 0, 0)),
        ],
        out_specs=[
            pl.BlockSpec((tq, g * HEAD_DIM), lambda b, h, i: (b * nq + i, h)),
            pl.BlockSpec((tq, LANES), lambda b, h, i: ((b * hk + h) * nq + i, 0)),
        ],
        out_shape=[
            jax.ShapeDtypeStruct((batch * seq, NSA_HEADS * HEAD_DIM), BF16),
            jax.ShapeDtypeStruct((batch * hk * seq, LANES), BF16),
        ],
        compiler_params=_cparams(("arbitrary", "arbitrary", "arbitrary")),
        name="nsa_cmp_topk",
    )(h, cmp_kv, cmp_kv)


def _lane_tile(x, n):
    return x if n == 1 else jnp.concatenate([x] * n, axis=1)


def _flash_step(s, m_ref, l_ref, acc_ref, v):
    tk = s.shape[1]
    m_prev = m_ref[...]
    m_new = jnp.maximum(m_prev, jnp.max(s, axis=1, keepdims=True))
    alpha = jnp.exp2(m_prev - m_new)
    p = jnp.exp2(s - _lane_tile(m_new, tk // LANES))
    l_ref[...] = alpha * l_ref[...] + jnp.sum(p, axis=1, keepdims=True)
    m_ref[...] = m_new
    acc_ref[...] = acc_ref[...] * _lane_tile(alpha, acc_ref.shape[1] // LANES) + _dot(p.astype(BF16), v)


def _flash_init(m_ref, l_ref, acc_ref):
    m_ref[...] = jnp.full(m_ref.shape, NEG, F32)
    l_ref[...] = jnp.zeros(l_ref.shape, F32)
    acc_ref[...] = jnp.zeros(acc_ref.shape, F32)


def _nsa_attn_kernel(q_ref, sel_ref, ocmp_ref, ks_ref, vs_ref, kw_ref, vw_ref, cos_ref, sin_ref, ga_ref,
                     gate_ref, o_ref, m_ref, l_ref, acc_ref, *, tq, tk):
    g = NSA_GROUP
    rows = g * tq
    i = pl.program_id(2)
    q0 = i * tq
    cos2 = jnp.concatenate([cos_ref[...]] * g, axis=0)
    sin2 = jnp.concatenate([sin_ref[...]] * g, axis=0)
    q4 = _rope_f32(_heads_to_rows(q_ref[...], g).astype(F32), cos2, sin2).astype(BF16)

    t_row = q0 + lax.broadcasted_iota(jnp.int32, (tq, tk), 0)
    k_lane = lax.broadcasted_iota(jnp.int32, (tq, tk), 1)
    sel = sel_ref[...]
    blk_row = lax.broadcasted_iota(jnp.int32, (LANES, tk), 0)
    blk_of_lane = lax.broadcasted_iota(jnp.int32, (LANES, tk), 1) >> SEL_SHIFT
    _flash_init(m_ref, l_ref, acc_ref)

    def sel_body(j, _):
        k0 = pl.multiple_of(j * tk, tk)
        s = _dot_nt(q4, ks_ref[pl.ds(k0, tk), :])
        expand = jnp.where(blk_row == j * (tk // SEL_LEN) + blk_of_lane, 1.0, 0.0).astype(BF16)
        picked = _dot(sel, expand)
        valid = (picked > 0.5) & (k0 + k_lane <= t_row)
        s = jnp.where(valid[None], s.reshape(g, tq, tk), NEG).reshape(rows, tk)
        _flash_step(s, m_ref, l_ref, acc_ref, vs_ref[pl.ds(k0, tk), :])
        return 0

    lax.fori_loop(0, (q0 + tq - 1) // tk + 1, sel_body, 0)
    o_sel = acc_ref[...] / l_ref[...]

    span = WIN + tq
    w0 = pl.multiple_of(jnp.maximum(q0 - WIN, 0), tq)
    s = _dot_nt(q4, kw_ref[pl.ds(w0, span), :])
    dist = (q0 + lax.broadcasted_iota(jnp.int32, (tq, span), 0)) - (w0 + lax.broadcasted_iota(jnp.int32, (tq, span), 1))
    valid = (dist >= 0) & (dist < WIN)
    s = jnp.where(valid[None], s.reshape(g, tq, span), NEG).reshape(rows, span)
    p = jnp.exp2(s - jnp.max(s, axis=1, keepdims=True))
    o_win = _dot(p.astype(BF16), vw_ref[pl.ds(w0, span), :]) / jnp.sum(p, axis=1, keepdims=True)

    gt = _sigmoid(ga_ref[...].astype(F32))
    o_cmp = ocmp_ref[...].astype(F32)
    gate = _silu(gate_ref[...].astype(F32))
    for a in range(g):
        cols = slice(a * HEAD_DIM, (a + 1) * HEAD_DIM)
        head = slice(a * tq, (a + 1) * tq)
        mix = (gt[:, 3 * a:3 * a + 1] * o_cmp[:, cols] + gt[:, 3 * a + 1:3 * a + 2] * o_sel[head]
               + gt[:, 3 * a + 2:3 * a + 3] * o_win[head])
        o_ref[:, cols] = (mix * gate[:, cols]).astype(o_ref.dtype)


def _nsa_attn(h, sel, ocmp, cos2, sin2, batch, seq, tq, tk):
    hk, g = NSA_KV_HEADS, NSA_GROUP
    nq = seq // tq
    rows = g * tq
    assert WIN % tq == 0 and tk % SEL_LEN == 0 and seq % tk == 0 and seq >= WIN + tq and tk % tq == 0
    slab = lambda col: pl.BlockSpec((seq, HEAD_DIM), lambda b, h_, i: (b, col + h_))
    return pl.pallas_call(
        functools.partial(_nsa_attn_kernel, tq=tq, tk=tk),
        grid=(batch, hk, nq),
        in_specs=[
            pl.BlockSpec((tq, g * HEAD_DIM), lambda b, h_, i: (b * nq + i, EV_QA // g + h_)),
            pl.BlockSpec((tq, LANES), lambda b, h_, i: ((b * hk + h_) * nq + i, 0)),
            pl.BlockSpec((tq, g * HEAD_DIM), lambda b, h_, i: (b * nq + i, h_)),
            slab(EV_KS), slab(EV_VS), slab(EV_KW), slab(EV_VW),
            pl.BlockSpec((tq, LANES), lambda b, h_, i: (i, 0)),
            pl.BlockSpec((tq, LANES), lambda b, h_, i: (i, 0)),
            pl.BlockSpec((tq, LANES), lambda b, h_, i: (b * nq + i, EV_GA + h_)),
            pl.BlockSpec((tq, g * HEAD_DIM), lambda b, h_, i: (b * nq + i, EV_GATE_A // g + h_)),
        ],
        out_specs=pl.BlockSpec((tq, g * HEAD_DIM), lambda b, h_, i: (b * nq + i, h_)),
        out_shape=jax.ShapeDtypeStruct((batch * seq, NSA_HEADS * HEAD_DIM), BF16),
        scratch_shapes=[pltpu.VMEM((rows, LANES), F32), pltpu.VMEM((rows, LANES), F32),
                        pltpu.VMEM((rows, HEAD_DIM), F32)],
        compiler_params=_cparams(("arbitrary", "arbitrary", "arbitrary")),
        name="nsa_attn",
    )(h, sel, ocmp, h, h, h, h, cos2, sin2, h, h)


def _sb_kernel(q_ref, k_ref, v_ref, gate_ref, tri_ref, o_ref, later_ref, acc_ref, *, t):
    i = pl.program_id(2)
    q = q_ref[...]
    n_grp = t // LANES
    strict = lax.broadcasted_iota(jnp.int32, (t, t), 1) < lax.broadcasted_iota(jnp.int32, (t, t), 0)
    tri = tri_ref[...]
    later_ref[...] = jnp.zeros(later_ref.shape, F32)
    acc_ref[...] = jnp.zeros(acc_ref.shape, F32)

    def tile(j, mask):
        k0 = pl.multiple_of(j * t, t)
        z = _dot_nt(q, k_ref[pl.ds(k0, t), :])
        sp = jnp.maximum(z, 0.0) + jnp.log(1.0 + jnp.exp2(jnp.minimum(z, -z))) * LOG2E
        if mask is not None:
            sp = jnp.where(mask, sp, 0.0)
        hi = sp.astype(BF16)
        lo = (sp - hi.astype(F32)).astype(BF16)
        run = later_ref[...]
        parts = [None] * n_grp
        for grp in reversed(range(n_grp)):
            cols = slice(grp * LANES, (grp + 1) * LANES)
            after = _dot(hi[:, cols], tri) + _dot(lo[:, cols], tri)
            a = jnp.exp2(z[:, cols] - sp[:, cols] - after - run)
            if mask is not None:
                a = jnp.where(mask[:, cols], a, 0.0)
            parts[grp] = a.astype(BF16)
            run = run + (after[:, 0:1] + sp[:, grp * LANES:grp * LANES + 1])
        later_ref[...] = run
        acc_ref[...] += _dot(jnp.concatenate(parts, axis=1), v_ref[pl.ds(k0, t), :])

    tile(i, strict)

    def body(jj, _):
        tile(i - 1 - jj, None)
        return 0

    lax.fori_loop(0, i, body, 0)
    o_ref[...] = (acc_ref[...] * _silu(gate_ref[...].astype(F32))).astype(o_ref.dtype)


def _stick_breaking(h, batch, seq, t):
    nh = SB_HEADS
    nq = seq // t
    tri = (lax.broadcasted_iota(jnp.int32, (LANES, LANES), 0)
           > lax.broadcasted_iota(jnp.int32, (LANES, LANES), 1)).astype(BF16)
    return pl.pallas_call(
        functools.partial(_sb_kernel, t=t),
        grid=(batch, nh, nq),
        in_specs=[
            pl.BlockSpec((t, HEAD_DIM), lambda b, h_, i: (b * nq + i, EV_QB + h_)),
            pl.BlockSpec((seq, HEAD_DIM), lambda b, h_, i: (b, EV_KB + h_)),
            pl.BlockSpec((seq, HEAD_DIM), lambda b, h_, i: (b, EV_VB + h_)),
            pl.BlockSpec((t, HEAD_DIM), lambda b, h_, i: (b * nq + i, EV_GATE_B + h_)),
            pl.BlockSpec((LANES, LANES), lambda b, h_, i: (0, 0)),
        ],
        out_specs=pl.BlockSpec((t, HEAD_DIM), lambda b, h_, i: (b * nq + i, h_)),
        out_shape=jax.ShapeDtypeStruct((batch * seq, nh * HEAD_DIM), BF16),
        scratch_shapes=[pltpu.VMEM((t, LANES), F32), pltpu.VMEM((t, HEAD_DIM), F32)],
        compiler_params=_cparams(("arbitrary", "arbitrary", "arbitrary")),
        name="stick_breaking",
    )(h, h, h, h, tri)


def _diff_kernel(q_ref, k_ref, v_ref, gate_ref, gn_ref, lq1_ref, lk1_ref, lq2_ref, lk2_ref, o_ref,
                 m_ref, l_ref, acc_ref, *, t, lambda_init):
    d = HEAD_DIM
    i = pl.program_id(2)
    q = q_ref[...]
    causal = lax.broadcasted_iota(jnp.int32, (t, t), 1) <= lax.broadcasted_iota(jnp.int32, (t, t), 0)
    for c in range(2):
        _flash_init(m_ref.at[c], l_ref.at[c], acc_ref.at[c])

    def tile(j, mask):
        k0 = pl.multiple_of(j * t, t)
        vv = v_ref[pl.ds(k0, t), :]
        for c in range(2):
            s = _dot_nt(q[:, c * d:(c + 1) * d], k_ref[pl.ds(k0, t), c * d:(c + 1) * d])
            if mask is not None:
                s = jnp.where(mask, s, NEG)
            _flash_step(s, m_ref.at[c], l_ref.at[c], acc_ref.at[c], vv)

    tile(i, causal)

    def body(j, _):
        tile(j, None)
        return 0

    lax.fori_loop(0, i, body, 0)
    lam = (jnp.exp(jnp.sum(lq1_ref[...] * lk1_ref[...], axis=-1, keepdims=True))
           - jnp.exp(jnp.sum(lq2_ref[...] * lk2_ref[...], axis=-1, keepdims=True)) + lambda_init)
    inv = [1.0 / _lane_tile(l_ref[c], 2) for c in range(2)]
    o = acc_ref[0] * inv[0] - lam * (acc_ref[1] * inv[1])
    o = o * lax.rsqrt(jnp.mean(o * o, axis=-1, keepdims=True) + RMS_EPS)
    o = o * gn_ref[...] * (1.0 - lambda_init)
    o_ref[...] = (o * _silu(gate_ref[...].astype(F32))).astype(o_ref.dtype)


def _diff_attn(h, gn_g, lq1, lk1, lq2, lk2, lambda_init, batch, seq, t):
    nh = DIFF_HEADS
    nq = seq // t
    w = 2 * HEAD_DIM
    vec = pl.BlockSpec((1, HEAD_DIM), lambda b, h_, i: (0, 0))
    r = lambda a: a.reshape(1, HEAD_DIM)
    return pl.pallas_call(
        functools.partial(_diff_kernel, t=t, lambda_init=lambda_init),
        grid=(batch, nh, nq),
        in_specs=[
            pl.BlockSpec((t, w), lambda b, h_, i: (b * nq + i, h_)),
            pl.BlockSpec((seq, w), lambda b, h_, i: (b, nh + h_)),
            pl.BlockSpec((seq, w), lambda b, h_, i: (b, 2 * nh + h_)),
            pl.BlockSpec((t, w), lambda b, h_, i: (b * nq + i, 3 * nh + h_)),
            pl.BlockSpec((None, 1, w), lambda b, h_, i: (h_, 0, 0)),
            vec, vec, vec, vec,
        ],
        out_specs=pl.BlockSpec((t, w), lambda b, h_, i: (b * nq + i, h_)),
        out_shape=jax.ShapeDtypeStruct((batch * seq, nh * w), BF16),
        scratch_shapes=[pltpu.VMEM((2, t, LANES), F32), pltpu.VMEM((2, t, LANES), F32),
                        pltpu.VMEM((2, t, w), F32)],
        compiler_params=_cparams(("arbitrary", "arbitrary", "arbitrary")),
        name="diff_attn",
    )(h, h, h, h, gn_g.reshape(nh, 1, w), r(lq1), r(lk1), r(lq2), r(lk2))


def _rope_tables(seq):
    pos = jnp.arange(seq, dtype=F32)
    inv = ROPE_THETA ** (-jnp.arange(0, HEAD_DIM, 2, dtype=F32) / HEAD_DIM)
    ang = pos[:, None] * inv[None, :]
    cos, sin = jnp.cos(ang), jnp.sin(ang)
    return jnp.concatenate([cos, cos], axis=-1), jnp.concatenate([-sin, sin], axis=-1)


def _even_weight(w_in):
    hk, g, d = NSA_KV_HEADS, NSA_GROUP, HEAD_DIM
    a_q, a_kv, b_w = NSA_HEADS * d, hk * d, SB_HEADS * d
    sizes = (a_q, a_kv, a_kv, a_kv, a_kv, a_kv, a_kv, 3 * NSA_HEADS, a_q, b_w, b_w, b_w, b_w)
    offs = [0]
    for sz in sizes:
        offs.append(offs[-1] + sz)
    w_in = w_in.astype(BF16)
    col = lambda idx: w_in[:, offs[idx]:offs[idx + 1]]
    qa, kc, vc, ks, vs, kw, vw, ga, gate_a, qb, kb, vb, gate_b = [col(i) for i in range(13)]
    zeros = lambda n: jnp.zeros((w_in.shape[0], n), w_in.dtype)
    ga_blocks = []
    for h in range(hk):
        ga_blocks += [ga[:, h * 3 * g:(h + 1) * 3 * g], zeros(LANES - 3 * g)]
    ga_blocks.append(zeros((EV_GATE_A - EV_GA - hk) * LANES))
    w = jnp.concatenate([qa, kc, vc, ks, kw, vs, vw] + ga_blocks + [gate_a, qb, kb, vb, gate_b], axis=1)
    assert w.shape[1] == EV_WIDTH
    return w


def _even_tile_modes(tn):
    modes = [(False, False)] * (EV_WIDTH // LANES)
    for c in range(EV_QA, EV_QA + NSA_HEADS):
        modes[c] = (False, True)
    for c in range(EV_KS, EV_KS + 2 * NSA_KV_HEADS):
        modes[c] = (True, False)
    for c in range(EV_QB, EV_QB + SB_HEADS):
        modes[c] = (False, True)
    per = tn // LANES
    return tuple(tuple(modes[i * per:(i + 1) * per]) for i in range(len(modes) // per))


def _odd_tile_modes(tn):
    nh = 2 * DIFF_HEADS
    modes = [(True, True)] * nh + [(True, False)] * nh + [(False, False)] * (2 * nh)
    per = tn // LANES
    return tuple(tuple(modes[i * per:(i + 1) * per]) for i in range(len(modes) // per))


TM_PROJ, TN_PROJ = 1024, 1024
TM_OUT = 256
TQ_NSA = 128
TK_NSA = 512
T_SB = 512
T_DIFF = 512


def _even_layer(x2, batch, seq, w_in, pe_k, pe_v, w1k, w2k, w1v, w2v, w_out, ln_g, ln_b, cos2, sin2):
    d = HEAD_DIM
    tm = min(TM_PROJ, seq)
    h = _inproj(x2, _even_weight(w_in), cos2, sin2, _even_tile_modes(TN_PROJ), tm, TN_PROJ)
    pe = jnp.stack([pe_k, pe_v])
    w1 = jnp.stack([w1k, w1v]).reshape(2, CMP_LEN, d, d).astype(BF16)
    w2 = jnp.stack([w2k, w2v]).astype(BF16)
    cmp_kv = _compress(h, pe, w1, w2, batch, seq)
    ocmp, sel = _cmp_topk(h, cmp_kv, batch, seq, TQ_NSA)
    o_a = _nsa_attn(h, sel, ocmp, cos2, sin2, batch, seq, TQ_NSA, min(TK_NSA, seq))
    o_b = _stick_breaking(h, batch, seq, min(T_SB, seq))
    return _outproj_ln([o_a, o_b], w_out.astype(BF16), x2, ln_g, ln_b, TM_OUT)


def _odd_layer(x2, batch, seq, w_in, lq1, lk1, lq2, lk2, gn_g, w_out, ln_g, ln_b, lambda_init, cos2, sin2):
    tm = min(TM_PROJ, seq)
    h = _inproj(x2, w_in.astype(BF16), cos2, sin2, _odd_tile_modes(TN_PROJ), tm, TN_PROJ)
    o = _diff_attn(h, gn_g, lq1, lk1, lq2, lk2, lambda_init, batch, seq, min(T_DIFF, seq))
    return _outproj_ln([o], w_out.astype(BF16), x2, ln_g, ln_b, TM_OUT)


def kernel(x, ev_w_in, ev_pe_k, ev_pe_v, ev_w1_k, ev_w2_k, ev_w1_v, ev_w2_v, ev_w_out, ev_ln_g, ev_ln_b,
           od_w_in, od_lq1, od_lk1, od_lq2, od_lk2, od_gn_g, od_w_out, od_ln_g, od_ln_b):
    batch, seq, dm = x.shape
    cos2, sin2 = _rope_tables(seq)
    x2 = x.reshape(batch * seq, dm)
    for layer in range(DEPTH):
        i = layer // 2
        if layer % 2 == 0:
            x2 = _even_layer(x2, batch, seq, ev_w_in[i], ev_pe_k[i], ev_pe_v[i], ev_w1_k[i], ev_w2_k[i],
                             ev_w1_v[i], ev_w2_v[i], ev_w_out[i], ev_ln_g[i], ev_ln_b[i], cos2, sin2)
        else:
            lambda_init = 0.8 - 0.6 * math.exp(-0.3 * layer)
            x2 = _odd_layer(x2, batch, seq, od_w_in[i], od_lq1[i], od_lk1[i], od_lq2[i], od_lk2[i], od_gn_g[i],
                            od_w_out[i], od_ln_g[i], od_ln_b[i], lambda_init, cos2, sin2)
    return x2.reshape(batch, seq, dm)
```

```python
import functools
import math

import jax
import jax.numpy as jnp
from jax import lax
from jax.experimental import pallas as pl
from jax.experimental.pallas import tpu as pltpu

F32 = jnp.float32
BF16 = jnp.bfloat16

HEAD_DIM = 128
ROPE_THETA = 10000.0
NSA_HEADS = 8
NSA_KV_HEADS = 2
NSA_GROUP = NSA_HEADS // NSA_KV_HEADS
CMP_LEN = 32
CMP_STRIDE = 16
SEL_LEN = 64
SEL_SHIFT = SEL_LEN.bit_length() - 1
SEL_TOPK = 16
WIN = 512
SEL_FORCE = 1.0e4
SB_HEADS = 8
DIFF_HEADS = 8
DEPTH = 2
DEEPNORM_ALPHA = (2.0 * DEPTH) ** 0.25
LN_EPS = 1e-5
RMS_EPS = 1e-5
LOG2E = math.log2(math.e)
SCALE = HEAD_DIM ** -0.5 * LOG2E

LANES = 128
NEG = -1.0e30
SOFTMAX_CHUNK = 64
VMEM_LIMIT = 56 * 1024 * 1024

EV_QA, EV_KC, EV_VC, EV_KS, EV_KW, EV_VS, EV_VW, EV_GA = 0, 8, 10, 12, 14, 16, 18, 20
EV_GATE_A, EV_QB, EV_KB, EV_VB, EV_GATE_B = 24, 32, 40, 48, 56
EV_WIDTH = 64 * LANES


def _cparams(sem):
    return pltpu.CompilerParams(dimension_semantics=sem, vmem_limit_bytes=VMEM_LIMIT)


def _dot_nt(a, b):
    return lax.dot_general(a, b, (((1,), (1,)), ((), ())), preferred_element_type=F32)


def _dot(a, b):
    return jnp.dot(a, b, preferred_element_type=F32)


def _silu(x):
    return x * (1.0 / (1.0 + jnp.exp(-x)))


def _sigmoid(x):
    return 1.0 / (1.0 + jnp.exp(-x))


def _rope_f32(x, cos2, sin2):
    return x * cos2 + pltpu.roll(x, HEAD_DIM // 2, axis=1) * sin2


def _inproj_kernel(x_ref, w_ref, cos_ref, sin_ref, o_ref, xb_ref, *, tile_modes):
    n = pl.program_id(1)

    @pl.when(n == 0)
    def _():
        xb_ref[...] = x_ref[...].astype(BF16)

    acc = _dot(xb_ref[...], w_ref[...])
    patterns = {}
    for tile, pat in enumerate(tile_modes):
        patterns.setdefault(pat, []).append(tile)
    for pat, tiles in patterns.items():
        cond = functools.reduce(jnp.logical_or, [n == t for t in tiles])

        @pl.when(cond)
        def _(pat=pat):
            for c, (rope, scale) in enumerate(pat):
                y = acc[:, c * LANES:(c + 1) * LANES]
                if rope:
                    y = _rope_f32(y, cos_ref[...], sin_ref[...])
                if scale:
                    y = y * SCALE
                o_ref[:, c * LANES:(c + 1) * LANES] = y.astype(o_ref.dtype)


def _inproj(x, w, cos2, sin2, tile_modes, tm, tn):
    m, k = x.shape
    n = w.shape[1]
    s = cos2.shape[0]
    assert m % tm == 0 and n % tn == 0 and s % tm == 0 and len(tile_modes) == n // tn
    pos_tiles = s // tm
    return pl.pallas_call(
        functools.partial(_inproj_kernel, tile_modes=tile_modes),
        grid=(m // tm, n // tn),
        in_specs=[
            pl.BlockSpec((tm, k), lambda i, j: (i, 0)),
            pl.BlockSpec((k, tn), lambda i, j: (0, j)),
            pl.BlockSpec((tm, LANES), lambda i, j: (i % pos_tiles, 0)),
            pl.BlockSpec((tm, LANES), lambda i, j: (i % pos_tiles, 0)),
        ],
        out_specs=pl.BlockSpec((tm, tn), lambda i, j: (i, j)),
        out_shape=jax.ShapeDtypeStruct((m, n), BF16),
        scratch_shapes=[pltpu.VMEM((tm, k), BF16)],
        compiler_params=_cparams(("arbitrary", "arbitrary")),
        name="inproj",
    )(x, w, cos2, sin2)


def _outproj_ln_kernel(*refs, n_parts):
    o_refs = refs[:n_parts]
    w_ref, x_ref, g_ref, b_ref, out_ref = refs[n_parts:]
    kp = w_ref.shape[0] // n_parts
    y = None
    for p in range(n_parts):
        part = _dot(o_refs[p][...], w_ref[p * kp:(p + 1) * kp, :])
        y = part if y is None else y + part
    z = DEEPNORM_ALPHA * x_ref[...] + y
    mu = jnp.mean(z, axis=-1, keepdims=True)
    zc = z - mu
    var = jnp.mean(zc * zc, axis=-1, keepdims=True)
    out_ref[...] = zc * lax.rsqrt(var + LN_EPS) * g_ref[...] + b_ref[...]


def _outproj_ln(parts, w, x, g, b, tm):
    m, d = x.shape
    n_parts = len(parts)
    kp = w.shape[0] // n_parts
    return pl.pallas_call(
        functools.partial(_outproj_ln_kernel, n_parts=n_parts),
        grid=(m // tm,),
        in_specs=[pl.BlockSpec((tm, kp), lambda i: (i, 0)) for _ in parts] + [
            pl.BlockSpec(w.shape, lambda i: (0, 0)),
            pl.BlockSpec((tm, d), lambda i: (i, 0)),
            pl.BlockSpec((1, d), lambda i: (0, 0)),
            pl.BlockSpec((1, d), lambda i: (0, 0)),
        ],
        out_specs=pl.BlockSpec((tm, d), lambda i: (i, 0)),
        out_shape=jax.ShapeDtypeStruct((m, d), F32),
        compiler_params=_cparams(("arbitrary",)),
        name="outproj_ln",
    )(*parts, w, x, g.reshape(1, d), b.reshape(1, d))


def _compress_kernel(kv_ref, pe_ref, w1_ref, w2_ref, o_ref, pad_ref, *, seq):
    n_rows = seq // CMP_STRIDE
    pad_ref[0:seq, :] = kv_ref[...].astype(F32)
    pad_ref[seq:seq + CMP_STRIDE, :] = jnp.zeros((CMP_STRIDE, HEAD_DIM), F32)
    pre = jnp.zeros((n_rows, HEAD_DIM), F32)
    for l in range(CMP_LEN):
        rows = pad_ref[pl.ds(l, n_rows, stride=CMP_STRIDE), :] + pe_ref[l:l + 1, :]
        pre = pre + _dot(rows.astype(BF16), w1_ref[l])
    o_ref[...] = _dot(_silu(pre).astype(BF16), w2_ref[...]).astype(o_ref.dtype)


def _compress(h, pe, w1, w2, batch, seq):
    n_rows = seq // CMP_STRIDE
    hk = NSA_KV_HEADS
    return pl.pallas_call(
        functools.partial(_compress_kernel, seq=seq),
        grid=(2, batch, hk),
        in_specs=[
            pl.BlockSpec((seq, HEAD_DIM), lambda c, b, h: (b, EV_KC + c * hk + h)),
            pl.BlockSpec((None, CMP_LEN, HEAD_DIM), lambda c, b, h: (c, 0, 0)),
            pl.BlockSpec((None, CMP_LEN, HEAD_DIM, HEAD_DIM), lambda c, b, h: (c, 0, 0, 0)),
            pl.BlockSpec((None, HEAD_DIM, HEAD_DIM), lambda c, b, h: (c, 0, 0)),
        ],
        out_specs=pl.BlockSpec((None, None, n_rows, HEAD_DIM), lambda c, b, h: (c, b * hk + h, 0, 0)),
        out_shape=jax.ShapeDtypeStruct((2, batch * hk, n_rows, HEAD_DIM), BF16),
        scratch_shapes=[pltpu.VMEM((seq + CMP_STRIDE, HEAD_DIM), F32)],
        compiler_params=_cparams(("arbitrary", "arbitrary", "arbitrary")),
        name="nsa_compress",
    )(h, pe, w1, w2)


def _heads_to_rows(x, g):
    return jnp.concatenate([x[:, i * HEAD_DIM:(i + 1) * HEAD_DIM] for i in range(g)], axis=0)


def _rows_to_heads(x, g):
    r = x.shape[0] // g
    return jnp.concatenate([x[i * r:(i + 1) * r, :] for i in range(g)], axis=1)


def _cmp_topk_kernel(q_ref, kc_ref, vc_ref, ocmp_ref, sel_ref, *, tq, n_sel_blk, n_sel):
    g = NSA_GROUP
    i = pl.program_id(2)
    q0 = i * tq
    n_rows = kc_ref.shape[0]
    q4 = _heads_to_rows(q_ref[...], g)
    sc = _dot_nt(q4, kc_ref[...]).reshape(g, tq, n_rows)
    t_col = q0 + lax.broadcasted_iota(jnp.int32, (tq, n_rows), 0)
    blk_end = lax.broadcasted_iota(jnp.int32, (tq, n_rows), 1) * CMP_STRIDE + (CMP_LEN - 1)
    vis = (blk_end <= t_col)[None]
    sm = jnp.where(vis, sc, NEG)
    mx = jnp.max(sm, axis=-1, keepdims=True)
    e = jnp.where(vis, jnp.exp2(sm - mx), 0.0)
    den = jnp.sum(e, axis=-1, keepdims=True)
    p = e / jnp.where(den > 0.0, den, 1.0)
    o = _dot(p.reshape(g * tq, n_rows).astype(BF16), vc_ref[...])
    ocmp_ref[...] = _rows_to_heads(o, g).astype(ocmp_ref.dtype)

    p_sum = jnp.sum(p, axis=0)
    jj = lax.broadcasted_iota(jnp.int32, (n_sel_blk, n_rows), 0) * SEL_LEN
    cs = lax.broadcasted_iota(jnp.int32, (n_sel_blk, n_rows), 1) * CMP_STRIDE
    ov_t = jnp.where((cs < jj + SEL_LEN) & (cs + CMP_LEN > jj), 1.0, 0.0).astype(BF16)
    p_hi = p_sum.astype(BF16)
    p_lo = (p_sum - p_hi.astype(F32)).astype(BF16)
    imp_t = _dot_nt(ov_t, p_hi) + _dot_nt(ov_t, p_lo)

    blk = lax.broadcasted_iota(jnp.int32, (n_sel_blk, tq), 0)
    j_cur = (q0 + lax.broadcasted_iota(jnp.int32, (n_sel_blk, tq), 1)) >> SEL_SHIFT
    forced = (blk == 0) | (blk == j_cur) | (blk == j_cur - 1)
    score = jnp.where(forced, SEL_FORCE, jnp.where(blk <= j_cur, imp_t, -SEL_FORCE))
    rank = jnp.zeros((n_sel_blk, tq), F32)
    for r in range(n_sel_blk):
        row = score[r:r + 1, :]
        ahead = (row > score) | ((row == score) & (blk > r))
        rank = rank + jnp.where(ahead, 1.0, 0.0)
    sel_t = jnp.where(rank < float(n_sel), 1.0, 0.0)
    if n_sel_blk < LANES:
        sel_t = jnp.concatenate([sel_t, jnp.zeros((LANES - n_sel_blk, tq), F32)], axis=0)
    sel_ref[...] = sel_t.T.astype(sel_ref.dtype)


def _cmp_topk(h, cmp_kv, batch, seq, tq):
    hk, g = NSA_KV_HEADS, NSA_GROUP
    nq = seq // tq
    n_rows = seq // CMP_STRIDE
    n_sel_blk = seq // SEL_LEN
    assert n_sel_blk <= LANES and tq == LANES
    n_sel = min(SEL_TOPK, n_sel_blk)
    return pl.pallas_call(
        functools.partial(_cmp_topk_kernel, tq=tq, n_sel_blk=n_sel_blk, n_sel=n_sel),
        grid=(batch, hk, nq),
        in_specs=[
            pl.BlockSpec((tq, g * HEAD_DIM), lambda b, h, i: (b * nq + i, EV_QA // g + h)),
            pl.BlockSpec((None, None, n_rows, HEAD_DIM), lambda b, h, i: (0, b * hk + h, 0, 0)),
            pl.BlockSpec((None, None, n_rows, HEAD_DIM), lambda b, h, i: (1, b * hk + h, 0, 0)),
        ],
        out_specs=[
            pl.BlockSpec((tq, g * HEAD_DIM), lambda b, h, i: (b * nq + i, h)),
            pl.BlockSpec((tq, LANES), lambda b, h, i: ((b * hk + h) * nq + i, 0)),
        ],
        out_shape=[
            jax.ShapeDtypeStruct((batch * seq, NSA_HEADS * HEAD_DIM), BF16),
            jax.ShapeDtypeStruct((batch * hk * seq, LANES), BF16),
        ],
        compiler_params=_cparams(("arbitrary", "arbitrary", "arbitrary")),
        name="nsa_cmp_topk",
    )(h, cmp_kv, cmp_kv)


def _lane_tile(x, n):
    return x if n == 1 else jnp.concatenate([x] * n, axis=1)


def _softmax_init(m_ref, l_ref):
    m_ref[...] = jnp.full(m_ref.shape, NEG, F32)
    l_ref[...] = jnp.zeros(l_ref.shape, F32)


def _softmax_rows(s_ref, p_ref, al_ref, m_ref, l_ref, chunk):
    rows, tk = s_ref.shape
    for r in range(rows // chunk):
        sl = slice(r * chunk, (r + 1) * chunk)
        s = s_ref[sl, :]
        m_prev = m_ref[sl, :]
        m_new = jnp.maximum(m_prev, jnp.max(s, axis=1, keepdims=True))
        alpha = jnp.exp2(m_prev - m_new)
        p = jnp.exp2(s - _lane_tile(m_new, tk // LANES))
        l_ref[sl, :] = alpha * l_ref[sl, :] + jnp.sum(p, axis=1, keepdims=True)
        m_ref[sl, :] = m_new
        al_ref[sl, :] = alpha
        p_ref[sl, :] = p.astype(BF16)


def _two_slot_pipeline(n_steps, produce, consume):
    produce(0, 0, True)

    def body(k, _):
        for parity in range(2):
            @pl.when(k % 2 == parity)
            def _(parity=parity):
                produce(1 - parity, k + 1, False)
                consume(parity, k)
        return 0

    lax.fori_loop(0, n_steps - 1, body, 0)
    last = n_steps - 1
    for parity in range(2):
        @pl.when(last % 2 == parity)
        def _(parity=parity):
            consume(parity, last)


def _nsa_attn_kernel(q_ref, sel0_ref, sel1_ref, ocmp_ref, ks_ref, vs_ref, kw_ref, vw_ref, cos_ref, sin_ref, ga_ref,
                     gate_ref, o_ref, m_ref, l_ref, al_ref, acc_ref, q4_ref, s_ref, p_ref, sw_ref, pw_ref,
                     ow_ref, *, tq, tk):
    g, hk = NSA_GROUP, NSA_KV_HEADS
    rows = g * tq
    i = pl.program_id(1)
    q0 = i * tq
    sel_refs = (sel0_ref, sel1_ref)
    kv_cols = lambda h: slice(h * HEAD_DIM, (h + 1) * HEAD_DIM)
    cos2 = jnp.concatenate([cos_ref[...]] * g, axis=0)
    sin2 = jnp.concatenate([sin_ref[...]] * g, axis=0)
    for h in range(hk):
        q_h = q_ref[:, h * g * HEAD_DIM:(h + 1) * g * HEAD_DIM]
        q4_ref[h] = _rope_f32(_heads_to_rows(q_h, g).astype(F32), cos2, sin2).astype(BF16)

    span = WIN + tq
    w0 = pl.multiple_of(jnp.maximum(q0 - WIN, 0), tq)
    dist = (q0 + lax.broadcasted_iota(jnp.int32, (tq, span), 0)) - (w0 + lax.broadcasted_iota(jnp.int32, (tq, span), 1))
    in_window = (dist >= 0) & (dist < WIN)
    _softmax_init(m_ref, l_ref)
    for h in range(hk):
        s = _dot_nt(q4_ref[h], kw_ref[pl.ds(w0, span), kv_cols(h)])
        sw_ref[h] = jnp.where(in_window[None], s.reshape(g, tq, span), NEG).reshape(rows, span)
    for h in range(hk):
        _softmax_rows(sw_ref.at[h], pw_ref.at[h], al_ref.at[h], m_ref.at[h], l_ref.at[h], SOFTMAX_CHUNK)
    for h in range(hk):
        ow_ref[h] = _dot(pw_ref[h], vw_ref[pl.ds(w0, span), kv_cols(h)]) / l_ref[h]

    _softmax_init(m_ref, l_ref)
    acc_ref[...] = jnp.zeros(acc_ref.shape, F32)

    def produce(slot, j, first):
        del first
        k0 = pl.multiple_of(j * tk, tk)
        blk_row = lax.broadcasted_iota(jnp.int32, (LANES, tk), 0)
        blk_of_lane = lax.broadcasted_iota(jnp.int32, (LANES, tk), 1) >> SEL_SHIFT
        expand = jnp.where(blk_row == j * (tk // SEL_LEN) + blk_of_lane, 1.0, 0.0).astype(BF16)
        causal = k0 + lax.broadcasted_iota(jnp.int32, (tq, tk), 1) <= q0 + lax.broadcasted_iota(jnp.int32, (tq, tk), 0)
        for h in range(hk):
            s = _dot_nt(q4_ref[h], ks_ref[pl.ds(k0, tk), kv_cols(h)])
            picked = _dot(sel_refs[h][...], expand)
            valid = (picked > 0.5) & causal
            s_ref[slot, h] = jnp.where(valid[None], s.reshape(g, tq, tk), NEG).reshape(rows, tk)

    def consume(slot, j):
        k0 = pl.multiple_of(j * tk, tk)
        for h in range(hk):
            _softmax_rows(s_ref.at[slot, h], p_ref.at[h], al_ref.at[h], m_ref.at[h], l_ref.at[h], SOFTMAX_CHUNK)
        for h in range(hk):
            acc_ref[h] = acc_ref[h] * al_ref[h] + _dot(p_ref[h], vs_ref[pl.ds(k0, tk), kv_cols(h)])

    _two_slot_pipeline((q0 + tq - 1) // tk + 1, produce, consume)

    gt = _sigmoid(ga_ref[...].astype(F32))
    for h in range(hk):
        for a in range(g):
            cols = slice((h * g + a) * HEAD_DIM, (h * g + a + 1) * HEAD_DIM)
            head = slice(a * tq, (a + 1) * tq)
            lane = h * LANES + 3 * a
            o_sel = acc_ref[h, head, :] / l_ref[h, head, :]
            mix = (gt[:, lane:lane + 1] * ocmp_ref[:, cols].astype(F32) + gt[:, lane + 1:lane + 2] * o_sel
                   + gt[:, lane + 2:lane + 3] * ow_ref[h, head, :])
            o_ref[:, cols] = (mix * _silu(gate_ref[:, cols].astype(F32))).astype(o_ref.dtype)


def _nsa_attn(h, sel, ocmp, cos2, sin2, batch, seq, tq, tk):
    hk, g = NSA_KV_HEADS, NSA_GROUP
    nq = seq // tq
    rows = g * tq
    wq = NSA_HEADS * HEAD_DIM
    wkv = hk * HEAD_DIM
    assert WIN % tq == 0 and tk % SEL_LEN == 0 and seq % tk == 0 and seq >= WIN + tq and tk % tq == 0 and hk == 2
    slab = lambda col: pl.BlockSpec((seq, wkv), lambda b, i: (b, col // hk))
    sel_spec = lambda hh: pl.BlockSpec((tq, LANES), lambda b, i: ((b * hk + hh) * nq + i, 0))
    return pl.pallas_call(
        functools.partial(_nsa_attn_kernel, tq=tq, tk=tk),
        grid=(batch, nq),
        in_specs=[
            pl.BlockSpec((tq, wq), lambda b, i: (b * nq + i, EV_QA * LANES // wq)),
            sel_spec(0), sel_spec(1),
            pl.BlockSpec((tq, wq), lambda b, i: (b * nq + i, 0)),
            slab(EV_KS), slab(EV_VS), slab(EV_KW), slab(EV_VW),
            pl.BlockSpec((tq, LANES), lambda b, i: (i, 0)),
            pl.BlockSpec((tq, LANES), lambda b, i: (i, 0)),
            pl.BlockSpec((tq, hk * LANES), lambda b, i: (b * nq + i, EV_GA // hk)),
            pl.BlockSpec((tq, wq), lambda b, i: (b * nq + i, EV_GATE_A * LANES // wq)),
        ],
        out_specs=pl.BlockSpec((tq, wq), lambda b, i: (b * nq + i, 0)),
        out_shape=jax.ShapeDtypeStruct((batch * seq, wq), BF16),
        scratch_shapes=[
            pltpu.VMEM((hk, rows, LANES), F32), pltpu.VMEM((hk, rows, LANES), F32), pltpu.VMEM((hk, rows, LANES), F32),
            pltpu.VMEM((hk, rows, HEAD_DIM), F32), pltpu.VMEM((hk, rows, HEAD_DIM), BF16),
            pltpu.VMEM((2, hk, rows, tk), F32), pltpu.VMEM((hk, rows, tk), BF16),
            pltpu.VMEM((hk, rows, WIN + tq), F32), pltpu.VMEM((hk, rows, WIN + tq), BF16),
            pltpu.VMEM((hk, rows, HEAD_DIM), F32),
        ],
        compiler_params=_cparams(("arbitrary", "arbitrary")),
        name="nsa_attn",
    )(h, sel, sel, ocmp, h, h, h, h, cos2, sin2, h, h)


def _sb_kernel(q_ref, k_ref, v_ref, gate_ref, tri_ref, o_ref, later_ref, acc_ref, z_ref, zs_ref, sp16_ref, sp0_ref,
               after_ref, a_ref, *, t):
    i = pl.program_id(2)
    chunk = SOFTMAX_CHUNK
    nh = acc_ref.shape[0]
    tb = tri_ref.shape[0]
    nb = t // tb
    later_ref[...] = jnp.zeros(later_ref.shape, F32)
    acc_ref[...] = jnp.zeros(acc_ref.shape, F32)
    head_cols = lambda h: slice(h * HEAD_DIM, (h + 1) * HEAD_DIM)

    def produce(slot, k, first):
        k0 = pl.multiple_of((i - k) * t, t)
        for h in range(nh):
            z = _dot_nt(q_ref[:, head_cols(h)], k_ref[pl.ds(k0, t), head_cols(h)])
            if first:
                strict = lax.broadcasted_iota(jnp.int32, (t, t), 1) < lax.broadcasted_iota(jnp.int32, (t, t), 0)
                z = jnp.where(strict, z, NEG)
            z_ref[slot, h] = z

    def consume(slot, k):
        k0 = pl.multiple_of((i - k) * t, t)
        for h in range(nh):
            for r in range(t // chunk):
                sl = slice(r * chunk, (r + 1) * chunk)
                z = z_ref[slot, h, sl, :]
                sp = jnp.maximum(z, 0.0) + jnp.log(1.0 + jnp.exp2(jnp.minimum(z, -z))) * LOG2E
                zs_ref[h, sl, :] = z - sp
                sp16_ref[h, sl, :] = sp.astype(BF16)
                sp0_ref[h, sl, :] = jnp.concatenate([sp[:, b * tb:b * tb + LANES] for b in range(nb)], axis=1)
        for h in range(nh):
            for b in range(nb):
                bc = slice(b * tb, (b + 1) * tb)
                after_ref[h, :, bc] = _dot(sp16_ref[h, :, bc], tri_ref[...])
        for h in range(nh):
            for r in range(t // chunk):
                sl = slice(r * chunk, (r + 1) * chunk)
                run = later_ref[h, sl, :]
                for b in reversed(range(nb)):
                    bc = slice(b * tb, (b + 1) * tb)
                    after = after_ref[h, sl, bc]
                    a_ref[h, sl, bc] = jnp.exp2(zs_ref[h, sl, bc] - after - _lane_tile(run, tb // LANES)).astype(BF16)
                    run = run + (after[:, 0:1] + sp0_ref[h, sl, b * LANES:b * LANES + 1])
                later_ref[h, sl, :] = run
            acc_ref[h] += _dot(a_ref[h], v_ref[pl.ds(k0, t), head_cols(h)])

    _two_slot_pipeline(i + 1, produce, consume)
    for h in range(nh):
        o_ref[:, head_cols(h)] = (acc_ref[h] * _silu(gate_ref[:, head_cols(h)].astype(F32))).astype(o_ref.dtype)


def _stick_breaking(h, batch, seq, t, nh, tb):
    nq = seq // t
    w = nh * HEAD_DIM
    assert SB_HEADS % nh == 0 and t % tb == 0
    tri = (lax.broadcasted_iota(jnp.int32, (tb, tb), 0) > lax.broadcasted_iota(jnp.int32, (tb, tb), 1)).astype(BF16)
    col = lambda first_block: first_block // nh
    return pl.pallas_call(
        functools.partial(_sb_kernel, t=t),
        grid=(batch, SB_HEADS // nh, nq),
        in_specs=[
            pl.BlockSpec((t, w), lambda b, h_, i: (b * nq + i, col(EV_QB) + h_)),
            pl.BlockSpec((seq, w), lambda b, h_, i: (b, col(EV_KB) + h_)),
            pl.BlockSpec((seq, w), lambda b, h_, i: (b, col(EV_VB) + h_)),
            pl.BlockSpec((t, w), lambda b, h_, i: (b * nq + i, col(EV_GATE_B) + h_)),
            pl.BlockSpec((tb, tb), lambda b, h_, i: (0, 0)),
        ],
        out_specs=pl.BlockSpec((t, w), lambda b, h_, i: (b * nq + i, h_)),
        out_shape=jax.ShapeDtypeStruct((batch * seq, SB_HEADS * HEAD_DIM), BF16),
        scratch_shapes=[
            pltpu.VMEM((nh, t, LANES), F32), pltpu.VMEM((nh, t, HEAD_DIM), F32), pltpu.VMEM((2, nh, t, t), F32),
            pltpu.VMEM((nh, t, t), F32), pltpu.VMEM((nh, t, t), BF16), pltpu.VMEM((nh, t, (t // tb) * LANES), F32),
            pltpu.VMEM((nh, t, t), F32), pltpu.VMEM((nh, t, t), BF16),
        ],
        compiler_params=_cparams(("arbitrary", "arbitrary", "arbitrary")),
        name="stick_breaking",
    )(h, h, h, h, tri)


def _diff_kernel(q_ref, k_ref, v_ref, gate_ref, gn_ref, lq1_ref, lk1_ref, lq2_ref, lk2_ref, o_ref,
                 m_ref, l_ref, acc_ref, s_ref, p_ref, al_ref, *, t, lambda_init):
    d = HEAD_DIM
    i = pl.program_id(2)
    n_str = m_ref.shape[0]
    _softmax_init(m_ref, l_ref)
    acc_ref[...] = jnp.zeros(acc_ref.shape, F32)

    def tile_of(k):
        return jnp.where(k == 0, i, k - 1)

    def produce(slot, k, first):
        k0 = pl.multiple_of(tile_of(k) * t, t)
        for u in range(n_str):
            s = _dot_nt(q_ref[:, u * d:(u + 1) * d], k_ref[pl.ds(k0, t), u * d:(u + 1) * d])
            if first:
                causal = lax.broadcasted_iota(jnp.int32, (t, t), 1) <= lax.broadcasted_iota(jnp.int32, (t, t), 0)
                s = jnp.where(causal, s, NEG)
            s_ref[slot, u] = s

    def consume(slot, k):
        k0 = pl.multiple_of(tile_of(k) * t, t)
        for u in range(n_str):
            _softmax_rows(s_ref.at[slot, u], p_ref.at[u], al_ref.at[u], m_ref.at[u], l_ref.at[u], SOFTMAX_CHUNK)
        for u in range(n_str):
            vv = v_ref[pl.ds(k0, t), (u // 2) * 2 * d:(u // 2 + 1) * 2 * d]
            acc_ref[u] = acc_ref[u] * _lane_tile(al_ref[u], 2) + _dot(p_ref[u], vv)

    _two_slot_pipeline(i + 1, produce, consume)
    lam = (jnp.exp(jnp.sum(lq1_ref[...] * lk1_ref[...], axis=-1, keepdims=True))
           - jnp.exp(jnp.sum(lq2_ref[...] * lk2_ref[...], axis=-1, keepdims=True)) + lambda_init)
    for hh in range(n_str // 2):
        cols = slice(hh * 2 * d, (hh + 1) * 2 * d)
        inv = [1.0 / _lane_tile(l_ref[2 * hh + c], 2) for c in range(2)]
        o = acc_ref[2 * hh] * inv[0] - lam * (acc_ref[2 * hh + 1] * inv[1])
        o = o * lax.rsqrt(jnp.mean(o * o, axis=-1, keepdims=True) + RMS_EPS)
        o = o * gn_ref[hh:hh + 1, :] * (1.0 - lambda_init)
        o_ref[:, cols] = (o * _silu(gate_ref[:, cols].astype(F32))).astype(o_ref.dtype)


def _diff_attn(h, gn_g, lq1, lk1, lq2, lk2, lambda_init, batch, seq, t, hps):
    nh = DIFF_HEADS
    nq = seq // t
    w = hps * 2 * HEAD_DIM
    ng = nh // hps
    assert nh % hps == 0
    vec = pl.BlockSpec((1, HEAD_DIM), lambda b, h_, i: (0, 0))
    r = lambda a: a.reshape(1, HEAD_DIM)
    return pl.pallas_call(
        functools.partial(_diff_kernel, t=t, lambda_init=lambda_init),
        grid=(batch, ng, nq),
        in_specs=[
            pl.BlockSpec((t, w), lambda b, h_, i: (b * nq + i, h_)),
            pl.BlockSpec((seq, w), lambda b, h_, i: (b, ng + h_)),
            pl.BlockSpec((seq, w), lambda b, h_, i: (b, 2 * ng + h_)),
            pl.BlockSpec((t, w), lambda b, h_, i: (b * nq + i, 3 * ng + h_)),
            pl.BlockSpec((None, hps, 2 * HEAD_DIM), lambda b, h_, i: (h_, 0, 0)),
            vec, vec, vec, vec,
        ],
        out_specs=pl.BlockSpec((t, w), lambda b, h_, i: (b * nq + i, h_)),
        out_shape=jax.ShapeDtypeStruct((batch * seq, nh * 2 * HEAD_DIM), BF16),
        scratch_shapes=[
            pltpu.VMEM((2 * hps, t, LANES), F32), pltpu.VMEM((2 * hps, t, LANES), F32),
            pltpu.VMEM((2 * hps, t, 2 * HEAD_DIM), F32), pltpu.VMEM((2, 2 * hps, t, t), F32),
            pltpu.VMEM((2 * hps, t, t), BF16), pltpu.VMEM((2 * hps, t, LANES), F32),
        ],
        compiler_params=_cparams(("arbitrary", "arbitrary", "arbitrary")),
        name="diff_attn",
    )(h, h, h, h, gn_g.reshape(ng, hps, 2 * HEAD_DIM), r(lq1), r(lk1), r(lq2), r(lk2))


def _rope_tables(seq):
    pos = jnp.arange(seq, dtype=F32)
    inv = ROPE_THETA ** (-jnp.arange(0, HEAD_DIM, 2, dtype=F32) / HEAD_DIM)
    ang = pos[:, None] * inv[None, :]
    cos, sin = jnp.cos(ang), jnp.sin(ang)
    return jnp.concatenate([cos, cos], axis=-1), jnp.concatenate([-sin, sin], axis=-1)


def _even_weight(w_in):
    hk, g, d = NSA_KV_HEADS, NSA_GROUP, HEAD_DIM
    a_q, a_kv, b_w = NSA_HEADS * d, hk * d, SB_HEADS * d
    sizes = (a_q, a_kv, a_kv, a_kv, a_kv, a_kv, a_kv, 3 * NSA_HEADS, a_q, b_w, b_w, b_w, b_w)
    offs = [0]
    for sz in sizes:
        offs.append(offs[-1] + sz)
    w_in = w_in.astype(BF16)
    col = lambda idx: w_in[:, offs[idx]:offs[idx + 1]]
    qa, kc, vc, ks, vs, kw, vw, ga, gate_a, qb, kb, vb, gate_b = [col(i) for i in range(13)]
    zeros = lambda n: jnp.zeros((w_in.shape[0], n), w_in.dtype)
    ga_blocks = []
    for h in range(hk):
        ga_blocks += [ga[:, h * 3 * g:(h + 1) * 3 * g], zeros(LANES - 3 * g)]
    ga_blocks.append(zeros((EV_GATE_A - EV_GA - hk) * LANES))
    w = jnp.concatenate([qa, kc, vc, ks, kw, vs, vw] + ga_blocks + [gate_a, qb, kb, vb, gate_b], axis=1)
    assert w.shape[1] == EV_WIDTH
    return w


def _even_tile_modes(tn):
    modes = [(False, False)] * (EV_WIDTH // LANES)
    for c in range(EV_QA, EV_QA + NSA_HEADS):
        modes[c] = (False, True)
    for c in range(EV_KS, EV_KS + 2 * NSA_KV_HEADS):
        modes[c] = (True, False)
    for c in range(EV_QB, EV_QB + SB_HEADS):
        modes[c] = (False, True)
    per = tn // LANES
    return tuple(tuple(modes[i * per:(i + 1) * per]) for i in range(len(modes) // per))


def _odd_tile_modes(tn):
    nh = 2 * DIFF_HEADS
    modes = [(True, True)] * nh + [(True, False)] * nh + [(False, False)] * (2 * nh)
    per = tn // LANES
    return tuple(tuple(modes[i * per:(i + 1) * per]) for i in range(len(modes) // per))


TM_PROJ, TN_PROJ = 1024, 1024
TM_OUT = 256
TQ_NSA = 128
TK_NSA = 512
T_SB = 512
SB_HEADS_PER_STEP = 2
SB_TRI_BLOCK = 256
T_DIFF = 512
DIFF_HEADS_PER_STEP = 1


def _even_layer(x2, batch, seq, w_in, pe_k, pe_v, w1k, w2k, w1v, w2v, w_out, ln_g, ln_b, cos2, sin2):
    d = HEAD_DIM
    tm = min(TM_PROJ, seq)
    h = _inproj(x2, _even_weight(w_in), cos2, sin2, _even_tile_modes(TN_PROJ), tm, TN_PROJ)
    pe = jnp.stack([pe_k, pe_v])
    w1 = jnp.stack([w1k, w1v]).reshape(2, CMP_LEN, d, d).astype(BF16)
    w2 = jnp.stack([w2k, w2v]).astype(BF16)
    cmp_kv = _compress(h, pe, w1, w2, batch, seq)
    ocmp, sel = _cmp_topk(h, cmp_kv, batch, seq, TQ_NSA)
    o_a = _nsa_attn(h, sel, ocmp, cos2, sin2, batch, seq, TQ_NSA, min(TK_NSA, seq))
    o_b = _stick_breaking(h, batch, seq, min(T_SB, seq), SB_HEADS_PER_STEP, SB_TRI_BLOCK)
    return _outproj_ln([o_a, o_b], w_out.astype(BF16), x2, ln_g, ln_b, TM_OUT)


def _odd_layer(x2, batch, seq, w_in, lq1, lk1, lq2, lk2, gn_g, w_out, ln_g, ln_b, lambda_init, cos2, sin2):
    tm = min(TM_PROJ, seq)
    h = _inproj(x2, w_in.astype(BF16), cos2, sin2, _odd_tile_modes(TN_PROJ), tm, TN_PROJ)
    o = _diff_attn(h, gn_g, lq1, lk1, lq2, lk2, lambda_init, batch, seq, min(T_DIFF, seq), DIFF_HEADS_PER_STEP)
    return _outproj_ln([o], w_out.astype(BF16), x2, ln_g, ln_b, TM_OUT)


def kernel(x, ev_w_in, ev_pe_k, ev_pe_v, ev_w1_k, ev_w2_k, ev_w1_v, ev_w2_v, ev_w_out, ev_ln_g, ev_ln_b,
           od_w_in, od_lq1, od_lk1, od_lq2, od_lk2, od_gn_g, od_w_out, od_ln_g, od_ln_b):
    batch, seq, dm = x.shape
    cos2, sin2 = _rope_tables(seq)
    x2 = x.reshape(batch * seq, dm)
    for layer in range(DEPTH):
        i = layer // 2
        if layer % 2 == 0:
            x2 = _even_layer(x2, batch, seq, ev_w_in[i], ev_pe_k[i], ev_pe_v[i], ev_w1_k[i], ev_w2_k[i],
                             ev_w1_v[i], ev_w2_v[i], ev_w_out[i], ev_ln_g[i], ev_ln_b[i], cos2, sin2)
        else:
            lambda_init = 0.8 - 0.6 * math.exp(-0.3 * layer)
            x2 = _odd_layer(x2, batch, seq, od_w_in[i], od_lq1[i], od_lk1[i], od_lq2[i], od_lk2[i], od_gn_g[i],
                            od_w_out[i], od_ln_g[i], od_ln_b[i], lambda_init, cos2, sin2)
    return x2.reshape(batch, seq, dm)
```

```python
import functools
import math

import jax
import jax.numpy as jnp
from jax import lax
from jax.experimental import pallas as pl
from jax.experimental.pallas import tpu as pltpu

F32 = jnp.float32
BF16 = jnp.bfloat16

HEAD_DIM = 128
ROPE_THETA = 10000.0
NSA_HEADS = 8
NSA_KV_HEADS = 2
NSA_GROUP = NSA_HEADS // NSA_KV_HEADS
CMP_LEN = 32
CMP_STRIDE = 16
SEL_LEN = 64
SEL_SHIFT = SEL_LEN.bit_length() - 1
SEL_TOPK = 16
WIN = 512
SEL_FORCE = 1.0e4
SB_HEADS = 8
DIFF_HEADS = 8
DEPTH = 2
DEEPNORM_ALPHA = (2.0 * DEPTH) ** 0.25
LN_EPS = 1e-5
RMS_EPS = 1e-5
LOG2E = math.log2(math.e)
SCALE = HEAD_DIM ** -0.5 * LOG2E

LANES = 128
NEG = -1.0e30
OUT_ROW_CHUNK = 256
SOFTMAX_CHUNK = 64
VMEM_LIMIT = 56 * 1024 * 1024

MXU_TILE = 256
EV_QA, EV_KC, EV_VC, EV_KS, EV_VS, EV_KW, EV_VW = 0, 8, 10, 12, 14, 16, 18
EV_GATE_A, EV_QB, EV_KB, EV_VB, EV_GATE_B, EV_GA = 20, 28, 36, 44, 52, 60
EV_WIDTH = 64 * LANES


def _cparams(sem):
    return pltpu.CompilerParams(dimension_semantics=sem, vmem_limit_bytes=VMEM_LIMIT)


def _dot_nt(a, b):
    return lax.dot_general(a, b, (((1,), (1,)), ((), ())), preferred_element_type=F32)


def _dot(a, b):
    return jnp.dot(a, b, preferred_element_type=F32)


def _silu(x):
    return x * (1.0 / (1.0 + jnp.exp(-x)))


def _sigmoid(x):
    return 1.0 / (1.0 + jnp.exp(-x))


def _rope_f32(x, cos2, sin2):
    return x * cos2 + pltpu.roll(x, HEAD_DIM // 2, axis=1) * sin2


def _inproj_kernel(x_ref, w_ref, cos_ref, sin_ref, o_ref, xb_ref, *, tile_modes):
    n = pl.program_id(1)

    @pl.when(n == 0)
    def _():
        xb_ref[...] = x_ref[...].astype(BF16)

    patterns = {}
    for tile, pat in enumerate(tile_modes):
        patterns.setdefault(pat, []).append(tile)
    per_chunk = MXU_TILE // LANES
    for pat, tiles in patterns.items():
        cond = functools.reduce(jnp.logical_or, [n == t for t in tiles])

        @pl.when(cond)
        def _(pat=pat):
            for cc in range(len(pat) // per_chunk):
                acc = _dot(xb_ref[...], w_ref[:, cc * MXU_TILE:(cc + 1) * MXU_TILE])
                for c2 in range(per_chunk):
                    c = cc * per_chunk + c2
                    rope, scale = pat[c]
                    y = acc[:, c2 * LANES:(c2 + 1) * LANES]
                    if rope:
                        y = _rope_f32(y, cos_ref[...], sin_ref[...])
                    if scale:
                        y = y * SCALE
                    o_ref[:, c * LANES:(c + 1) * LANES] = y.astype(o_ref.dtype)


def _inproj(x, w, cos2, sin2, tile_modes, tm, tn):
    m, k = x.shape
    n = w.shape[1]
    s = cos2.shape[0]
    assert m % tm == 0 and n % tn == 0 and s % tm == 0 and len(tile_modes) == n // tn
    pos_tiles = s // tm
    return pl.pallas_call(
        functools.partial(_inproj_kernel, tile_modes=tile_modes),
        grid=(m // tm, n // tn),
        in_specs=[
            pl.BlockSpec((tm, k), lambda i, j: (i, 0)),
            pl.BlockSpec((k, tn), lambda i, j: (0, j)),
            pl.BlockSpec((tm, LANES), lambda i, j: (i % pos_tiles, 0)),
            pl.BlockSpec((tm, LANES), lambda i, j: (i % pos_tiles, 0)),
        ],
        out_specs=pl.BlockSpec((tm, tn), lambda i, j: (i, j)),
        out_shape=jax.ShapeDtypeStruct((m, n), BF16),
        scratch_shapes=[pltpu.VMEM((tm, k), BF16)],
        compiler_params=_cparams(("arbitrary", "arbitrary")),
        name="inproj",
    )(x, w, cos2, sin2)


def _outproj_ln_kernel(*refs, n_parts):
    o_refs = refs[:n_parts]
    w_ref, x_ref, g_ref, b_ref, out_ref = refs[n_parts:]
    kp = w_ref.shape[0] // n_parts
    for r in range(x_ref.shape[0] // OUT_ROW_CHUNK):
        rows = slice(r * OUT_ROW_CHUNK, (r + 1) * OUT_ROW_CHUNK)
        y = None
        for p in range(n_parts):
            part = _dot(o_refs[p][rows, :], w_ref[p * kp:(p + 1) * kp, :])
            y = part if y is None else y + part
        z = DEEPNORM_ALPHA * x_ref[rows, :] + y
        mu = jnp.mean(z, axis=-1, keepdims=True)
        zc = z - mu
        var = jnp.mean(zc * zc, axis=-1, keepdims=True)
        out_ref[rows, :] = zc * lax.rsqrt(var + LN_EPS) * g_ref[...] + b_ref[...]


def _outproj_ln(parts, w, x, g, b, tm):
    m, d = x.shape
    n_parts = len(parts)
    kp = w.shape[0] // n_parts
    return pl.pallas_call(
        functools.partial(_outproj_ln_kernel, n_parts=n_parts),
        grid=(m // tm,),
        in_specs=[pl.BlockSpec((tm, kp), lambda i: (i, 0)) for _ in parts] + [
            pl.BlockSpec(w.shape, lambda i: (0, 0)),
            pl.BlockSpec((tm, d), lambda i: (i, 0)),
            pl.BlockSpec((1, d), lambda i: (0, 0)),
            pl.BlockSpec((1, d), lambda i: (0, 0)),
        ],
        out_specs=pl.BlockSpec((tm, d), lambda i: (i, 0)),
        out_shape=jax.ShapeDtypeStruct((m, d), F32),
        compiler_params=_cparams(("arbitrary",)),
        name="outproj_ln",
    )(*parts, w, x, g.reshape(1, d), b.reshape(1, d))


def _compress_kernel(kv_ref, pe_ref, w1_ref, w2_ref, o_ref, pad_ref, *, seq):
    n_rows = seq // CMP_STRIDE
    pad_ref[0:seq, :] = kv_ref[...].astype(F32)
    pad_ref[seq:seq + CMP_STRIDE, :] = jnp.zeros((CMP_STRIDE, HEAD_DIM), F32)
    pre = jnp.zeros((n_rows, HEAD_DIM), F32)
    for l in range(CMP_LEN):
        rows = pad_ref[pl.ds(l, n_rows, stride=CMP_STRIDE), :] + pe_ref[l:l + 1, :]
        pre = pre + _dot(rows.astype(BF16), w1_ref[l])
    o_ref[...] = _dot(_silu(pre).astype(BF16), w2_ref[...]).astype(o_ref.dtype)


def _compress(h, pe, w1, w2, batch, seq):
    n_rows = seq // CMP_STRIDE
    hk = NSA_KV_HEADS
    return pl.pallas_call(
        functools.partial(_compress_kernel, seq=seq),
        grid=(2, batch, hk),
        in_specs=[
            pl.BlockSpec((seq, HEAD_DIM), lambda c, b, h: (b, EV_KC + c * hk + h)),
            pl.BlockSpec((None, CMP_LEN, HEAD_DIM), lambda c, b, h: (c, 0, 0)),
            pl.BlockSpec((None, CMP_LEN, HEAD_DIM, HEAD_DIM), lambda c, b, h: (c, 0, 0, 0)),
            pl.BlockSpec((None, HEAD_DIM, HEAD_DIM), lambda c, b, h: (c, 0, 0)),
        ],
        out_specs=pl.BlockSpec((None, None, n_rows, HEAD_DIM), lambda c, b, h: (c, b * hk + h, 0, 0)),
        out_shape=jax.ShapeDtypeStruct((2, batch * hk, n_rows, HEAD_DIM), BF16),
        scratch_shapes=[pltpu.VMEM((seq + CMP_STRIDE, HEAD_DIM), F32)],
        compiler_params=_cparams(("arbitrary", "arbitrary", "arbitrary")),
        name="nsa_compress",
    )(h, pe, w1, w2)


def _heads_to_rows(x, g):
    return jnp.concatenate([x[:, i * HEAD_DIM:(i + 1) * HEAD_DIM] for i in range(g)], axis=0)


def _rows_to_heads(x, g):
    r = x.shape[0] // g
    return jnp.concatenate([x[i * r:(i + 1) * r, :] for i in range(g)], axis=1)


def _cmp_topk_kernel(q_ref, kc_ref, vc_ref, ocmp_ref, sel_ref, *, tq, n_sel_blk, n_sel):
    g = NSA_GROUP
    i = pl.program_id(2)
    q0 = i * tq
    n_rows = kc_ref.shape[0]
    q4 = _heads_to_rows(q_ref[...], g)
    sc = _dot_nt(q4, kc_ref[...]).reshape(g, tq, n_rows)
    t_col = q0 + lax.broadcasted_iota(jnp.int32, (tq, n_rows), 0)
    blk_end = lax.broadcasted_iota(jnp.int32, (tq, n_rows), 1) * CMP_STRIDE + (CMP_LEN - 1)
    vis = (blk_end <= t_col)[None]
    sm = jnp.where(vis, sc, NEG)
    mx = jnp.max(sm, axis=-1, keepdims=True)
    e = jnp.where(vis, jnp.exp2(sm - mx), 0.0)
    den = jnp.sum(e, axis=-1, keepdims=True)
    p = e / jnp.where(den > 0.0, den, 1.0)
    o = _dot(p.reshape(g * tq, n_rows).astype(BF16), vc_ref[...])
    ocmp_ref[...] = _rows_to_heads(o, g).astype(ocmp_ref.dtype)

    p_sum = jnp.sum(p, axis=0)
    jj = lax.broadcasted_iota(jnp.int32, (n_sel_blk, n_rows), 0) * SEL_LEN
    cs = lax.broadcasted_iota(jnp.int32, (n_sel_blk, n_rows), 1) * CMP_STRIDE
    ov_t = jnp.where((cs < jj + SEL_LEN) & (cs + CMP_LEN > jj), 1.0, 0.0).astype(BF16)
    p_hi = p_sum.astype(BF16)
    p_lo = (p_sum - p_hi.astype(F32)).astype(BF16)
    imp_t = _dot_nt(ov_t, p_hi) + _dot_nt(ov_t, p_lo)

    blk = lax.broadcasted_iota(jnp.int32, (n_sel_blk, tq), 0)
    j_cur = (q0 + lax.broadcasted_iota(jnp.int32, (n_sel_blk, tq), 1)) >> SEL_SHIFT
    forced = (blk == 0) | (blk == j_cur) | (blk == j_cur - 1)
    score = jnp.where(forced, SEL_FORCE, jnp.where(blk <= j_cur, imp_t, -SEL_FORCE))
    rank = jnp.zeros((n_sel_blk, tq), F32)
    for r in range(n_sel_blk):
        row = score[r:r + 1, :]
        ahead = (row > score) | ((row == score) & (blk > r))
        rank = rank + jnp.where(ahead, 1.0, 0.0)
    sel_t = jnp.where(rank < float(n_sel), 1.0, 0.0)
    if n_sel_blk < LANES:
        sel_t = jnp.concatenate([sel_t, jnp.zeros((LANES - n_sel_blk, tq), F32)], axis=0)
    sel_ref[...] = sel_t.T.astype(sel_ref.dtype)


def _cmp_topk(h, cmp_kv, batch, seq, tq):
    hk, g = NSA_KV_HEADS, NSA_GROUP
    nq = seq // tq
    n_rows = seq // CMP_STRIDE
    n_sel_blk = seq // SEL_LEN
    assert n_sel_blk <= LANES and tq == LANES
    n_sel = min(SEL_TOPK, n_sel_blk)
    return pl.pallas_call(
        functools.partial(_cmp_topk_kernel, tq=tq, n_sel_blk=n_sel_blk, n_sel=n_sel),
        grid=(batch, hk, nq),
        in_specs=[
            pl.BlockSpec((tq, g * HEAD_DIM), lambda b, h, i: (b * nq + i, EV_QA // g + h)),
            pl.BlockSpec((None, None, n_rows, HEAD_DIM), lambda b, h, i: (0, b * hk + h, 0, 0)),
            pl.BlockSpec((None, None, n_rows, HEAD_DIM), lambda b, h, i: (1, b * hk + h, 0, 0)),
        ],
        out_specs=[
            pl.BlockSpec((tq, g * HEAD_DIM), lambda b, h, i: (b * nq + i, h)),
            pl.BlockSpec((tq, LANES), lambda b, h, i: ((b * hk + h) * nq + i, 0)),
        ],
        out_shape=[
            jax.ShapeDtypeStruct((batch * seq, NSA_HEADS * HEAD_DIM), BF16),
            jax.ShapeDtypeStruct((batch * hk * seq, LANES), BF16),
        ],
        compiler_params=_cparams(("arbitrary", "arbitrary", "arbitrary")),
        name="nsa_cmp_topk",
    )(h, cmp_kv, cmp_kv)


def _lane_tile(x, n):
    return x if n == 1 else jnp.concatenate([x] * n, axis=1)


def _softmax_init(m_ref, l_ref):
    m_ref[...] = jnp.full(m_ref.shape, NEG, F32)
    l_ref[...] = jnp.zeros(l_ref.shape, F32)


def _softmax_rows(s_ref, p_ref, al_ref, m_ref, l_ref, chunk):
    rows, tk = s_ref.shape
    for r in range(rows // chunk):
        sl = slice(r * chunk, (r + 1) * chunk)
        s = s_ref[sl, :]
        m_prev = m_ref[sl, :]
        m_new = jnp.maximum(m_prev, jnp.max(s, axis=1, keepdims=True))
        alpha = jnp.exp2(m_prev - m_new)
        p = jnp.exp2(s - _lane_tile(m_new, tk // LANES))
        l_ref[sl, :] = alpha * l_ref[sl, :] + jnp.sum(p, axis=1, keepdims=True)
        m_ref[sl, :] = m_new
        al_ref[sl, :] = alpha
        p_ref[sl, :] = p.astype(BF16)


def _two_slot_pipeline(n_steps, produce, consume):
    produce(0, 0, True)

    def body(k, _):
        for parity in range(2):
            @pl.when(k % 2 == parity)
            def _(parity=parity):
                produce(1 - parity, k + 1, False)
                consume(parity, k)
        return 0

    lax.fori_loop(0, n_steps - 1, body, 0)
    last = n_steps - 1
    for parity in range(2):
        @pl.when(last % 2 == parity)
        def _(parity=parity):
            consume(parity, last)


def _nsa_attn_kernel(q_ref, sel0_ref, sel1_ref, ocmp_ref, ks_ref, vs_ref, kw_ref, vw_ref, cos_ref, sin_ref, ga_ref,
                     gate0_ref, gate1_ref, o_ref, m_ref, l_ref, al_ref, acc_ref, q4_ref, s_ref, p_ref, sw_ref,
                     pw_ref, ow_ref, *, tq, tk):
    g, hk = NSA_GROUP, NSA_KV_HEADS
    rows = g * tq
    i = pl.program_id(1)
    q0 = i * tq
    sel_refs = (sel0_ref, sel1_ref)
    gate_refs = (gate0_ref, gate1_ref)
    kv_cols = lambda h: slice(h * HEAD_DIM, (h + 1) * HEAD_DIM)
    cos2 = jnp.concatenate([cos_ref[...]] * g, axis=0)
    sin2 = jnp.concatenate([sin_ref[...]] * g, axis=0)
    for h in range(hk):
        q_h = q_ref[:, h * g * HEAD_DIM:(h + 1) * g * HEAD_DIM]
        q4_ref[h] = _rope_f32(_heads_to_rows(q_h, g).astype(F32), cos2, sin2).astype(BF16)

    span = WIN + tq
    w0 = pl.multiple_of(jnp.maximum(q0 - WIN, 0), tq)
    dist = (q0 + lax.broadcasted_iota(jnp.int32, (tq, span), 0)) - (w0 + lax.broadcasted_iota(jnp.int32, (tq, span), 1))
    in_window = (dist >= 0) & (dist < WIN)
    _softmax_init(m_ref, l_ref)
    for h in range(hk):
        s = _dot_nt(q4_ref[h], kw_ref[pl.ds(w0, span), kv_cols(h)])
        sw_ref[h] = jnp.where(in_window[None], s.reshape(g, tq, span), NEG).reshape(rows, span)
    for h in range(hk):
        _softmax_rows(sw_ref.at[h], pw_ref.at[h], al_ref.at[h], m_ref.at[h], l_ref.at[h], SOFTMAX_CHUNK)
    for h in range(hk):
        ow_ref[h] = _dot(pw_ref[h], vw_ref[pl.ds(w0, span), kv_cols(h)]) / l_ref[h]

    _softmax_init(m_ref, l_ref)
    acc_ref[...] = jnp.zeros(acc_ref.shape, F32)

    def produce(slot, j, first):
        del first
        k0 = pl.multiple_of(j * tk, tk)
        blk_row = lax.broadcasted_iota(jnp.int32, (LANES, tk), 0)
        blk_of_lane = lax.broadcasted_iota(jnp.int32, (LANES, tk), 1) >> SEL_SHIFT
        expand = jnp.where(blk_row == j * (tk // SEL_LEN) + blk_of_lane, 1.0, 0.0).astype(BF16)
        causal = k0 + lax.broadcasted_iota(jnp.int32, (tq, tk), 1) <= q0 + lax.broadcasted_iota(jnp.int32, (tq, tk), 0)
        for h in range(hk):
            s = _dot_nt(q4_ref[h], ks_ref[pl.ds(k0, tk), kv_cols(h)])
            picked = _dot(sel_refs[h][...], expand)
            valid = (picked > 0.5) & causal
            s_ref[slot, h] = jnp.where(valid[None], s.reshape(g, tq, tk), NEG).reshape(rows, tk)

    def consume(slot, j):
        k0 = pl.multiple_of(j * tk, tk)
        for h in range(hk):
            _softmax_rows(s_ref.at[slot, h], p_ref.at[h], al_ref.at[h], m_ref.at[h], l_ref.at[h], SOFTMAX_CHUNK)
        for h in range(hk):
            acc_ref[h] = acc_ref[h] * al_ref[h] + _dot(p_ref[h], vs_ref[pl.ds(k0, tk), kv_cols(h)])

    _two_slot_pipeline((q0 + tq - 1) // tk + 1, produce, consume)

    gt = _sigmoid(ga_ref[...].astype(F32))
    for h in range(hk):
        for a in range(g):
            cols = slice((h * g + a) * HEAD_DIM, (h * g + a + 1) * HEAD_DIM)
            head = slice(a * tq, (a + 1) * tq)
            lane = (h * g + a) * 3
            o_sel = acc_ref[h, head, :] / l_ref[h, head, :]
            mix = (gt[:, lane:lane + 1] * ocmp_ref[:, cols].astype(F32) + gt[:, lane + 1:lane + 2] * o_sel
                   + gt[:, lane + 2:lane + 3] * ow_ref[h, head, :])
            gate = _silu(gate_refs[h][:, a * HEAD_DIM:(a + 1) * HEAD_DIM].astype(F32))
            o_ref[:, cols] = (mix * gate).astype(o_ref.dtype)


def _nsa_attn(h, sel, ocmp, cos2, sin2, batch, seq, tq, tk):
    hk, g = NSA_KV_HEADS, NSA_GROUP
    nq = seq // tq
    rows = g * tq
    wq = NSA_HEADS * HEAD_DIM
    wkv = hk * HEAD_DIM
    assert WIN % tq == 0 and tk % SEL_LEN == 0 and seq % tk == 0 and seq >= WIN + tq and tk % tq == 0 and hk == 2
    slab = lambda col: pl.BlockSpec((seq, wkv), lambda b, i: (b, col // hk))
    sel_spec = lambda hh: pl.BlockSpec((tq, LANES), lambda b, i: ((b * hk + hh) * nq + i, 0))
    return pl.pallas_call(
        functools.partial(_nsa_attn_kernel, tq=tq, tk=tk),
        grid=(batch, nq),
        in_specs=[
            pl.BlockSpec((tq, wq), lambda b, i: (b * nq + i, EV_QA * LANES // wq)),
            sel_spec(0), sel_spec(1),
            pl.BlockSpec((tq, wq), lambda b, i: (b * nq + i, 0)),
            slab(EV_KS), slab(EV_VS), slab(EV_KW), slab(EV_VW),
            pl.BlockSpec((tq, LANES), lambda b, i: (i, 0)),
            pl.BlockSpec((tq, LANES), lambda b, i: (i, 0)),
            pl.BlockSpec((tq, LANES), lambda b, i: (b * nq + i, EV_GA)),
            pl.BlockSpec((tq, g * HEAD_DIM), lambda b, i: (b * nq + i, EV_GATE_A // g)),
            pl.BlockSpec((tq, g * HEAD_DIM), lambda b, i: (b * nq + i, EV_GATE_A // g + 1)),
        ],
        out_specs=pl.BlockSpec((tq, wq), lambda b, i: (b * nq + i, 0)),
        out_shape=jax.ShapeDtypeStruct((batch * seq, wq), BF16),
        scratch_shapes=[
            pltpu.VMEM((hk, rows, LANES), F32), pltpu.VMEM((hk, rows, LANES), F32), pltpu.VMEM((hk, rows, LANES), F32),
            pltpu.VMEM((hk, rows, HEAD_DIM), F32), pltpu.VMEM((hk, rows, HEAD_DIM), BF16),
            pltpu.VMEM((2, hk, rows, tk), F32), pltpu.VMEM((hk, rows, tk), BF16),
            pltpu.VMEM((hk, rows, WIN + tq), F32), pltpu.VMEM((hk, rows, WIN + tq), BF16),
            pltpu.VMEM((hk, rows, HEAD_DIM), F32),
        ],
        compiler_params=_cparams(("arbitrary", "arbitrary")),
        name="nsa_attn",
    )(h, sel, sel, ocmp, h, h, h, h, cos2, sin2, h, h, h)


def _sb_kernel(q_ref, k_ref, v_ref, gate_ref, tri_ref, o_ref, later_ref, acc_ref, z_ref, zs_ref, sp16_ref, sp0_ref,
               after_ref, a_ref, *, t):
    i = pl.program_id(2)
    chunk = SOFTMAX_CHUNK
    nh = acc_ref.shape[0]
    tb = tri_ref.shape[0]
    nb = t // tb
    later_ref[...] = jnp.zeros(later_ref.shape, F32)
    acc_ref[...] = jnp.zeros(acc_ref.shape, F32)
    head_cols = lambda h: slice(h * HEAD_DIM, (h + 1) * HEAD_DIM)

    def produce(slot, k, first):
        k0 = pl.multiple_of((i - k) * t, t)
        for h in range(nh):
            z = _dot_nt(q_ref[:, head_cols(h)], k_ref[pl.ds(k0, t), head_cols(h)])
            if first:
                strict = lax.broadcasted_iota(jnp.int32, (t, t), 1) < lax.broadcasted_iota(jnp.int32, (t, t), 0)
                z = jnp.where(strict, z, NEG)
            z_ref[slot, h] = z

    def consume(slot, k):
        k0 = pl.multiple_of((i - k) * t, t)
        for h in range(nh):
            for r in range(t // chunk):
                sl = slice(r * chunk, (r + 1) * chunk)
                z = z_ref[slot, h, sl, :]
                sp = jnp.maximum(z, 0.0) + jnp.log(1.0 + jnp.exp2(jnp.minimum(z, -z))) * LOG2E
                zs_ref[h, sl, :] = z - sp
                sp16_ref[h, sl, :] = sp.astype(BF16)
                sp0_ref[h, sl, :] = jnp.concatenate([sp[:, b * tb:b * tb + LANES] for b in range(nb)], axis=1)
        for h in range(nh):
            for b in range(nb):
                bc = slice(b * tb, (b + 1) * tb)
                after_ref[h, :, bc] = _dot(sp16_ref[h, :, bc], tri_ref[...])
        for h in range(nh):
            for r in range(t // chunk):
                sl = slice(r * chunk, (r + 1) * chunk)
                run = later_ref[h, sl, :]
                for b in reversed(range(nb)):
                    bc = slice(b * tb, (b + 1) * tb)
                    after = after_ref[h, sl, bc]
                    a_ref[h, sl, bc] = jnp.exp2(zs_ref[h, sl, bc] - after - _lane_tile(run, tb // LANES)).astype(BF16)
                    run = run + (after[:, 0:1] + sp0_ref[h, sl, b * LANES:b * LANES + 1])
                later_ref[h, sl, :] = run
            acc_ref[h] += _dot(a_ref[h], v_ref[pl.ds(k0, t), head_cols(h)])

    _two_slot_pipeline(i + 1, produce, consume)
    for h in range(nh):
        o_ref[:, head_cols(h)] = (acc_ref[h] * _silu(gate_ref[:, head_cols(h)].astype(F32))).astype(o_ref.dtype)


def _stick_breaking(h, batch, seq, t, nh, tb):
    nq = seq // t
    w = nh * HEAD_DIM
    assert SB_HEADS % nh == 0 and t % tb == 0
    tri = (lax.broadcasted_iota(jnp.int32, (tb, tb), 0) > lax.broadcasted_iota(jnp.int32, (tb, tb), 1)).astype(BF16)
    col = lambda first_block: first_block // nh
    return pl.pallas_call(
        functools.partial(_sb_kernel, t=t),
        grid=(batch, SB_HEADS // nh, nq),
        in_specs=[
            pl.BlockSpec((t, w), lambda b, h_, i: (b * nq + i, col(EV_QB) + h_)),
            pl.BlockSpec((seq, w), lambda b, h_, i: (b, col(EV_KB) + h_)),
            pl.BlockSpec((seq, w), lambda b, h_, i: (b, col(EV_VB) + h_)),
            pl.BlockSpec((t, w), lambda b, h_, i: (b * nq + i, col(EV_GATE_B) + h_)),
            pl.BlockSpec((tb, tb), lambda b, h_, i: (0, 0)),
        ],
        out_specs=pl.BlockSpec((t, w), lambda b, h_, i: (b * nq + i, h_)),
        out_shape=jax.ShapeDtypeStruct((batch * seq, SB_HEADS * HEAD_DIM), BF16),
        scratch_shapes=[
            pltpu.VMEM((nh, t, LANES), F32), pltpu.VMEM((nh, t, HEAD_DIM), F32), pltpu.VMEM((2, nh, t, t), F32),
            pltpu.VMEM((nh, t, t), F32), pltpu.VMEM((nh, t, t), BF16), pltpu.VMEM((nh, t, (t // tb) * LANES), F32),
            pltpu.VMEM((nh, t, t), F32), pltpu.VMEM((nh, t, t), BF16),
        ],
        compiler_params=_cparams(("arbitrary", "arbitrary", "arbitrary")),
        name="stick_breaking",
    )(h, h, h, h, tri)


def _diff_kernel(q_ref, k_ref, v_ref, gate_ref, gn_ref, lq1_ref, lk1_ref, lq2_ref, lk2_ref, o_ref,
                 m_ref, l_ref, acc_ref, s_ref, p_ref, al_ref, *, t, lambda_init):
    d = HEAD_DIM
    i = pl.program_id(2)
    n_str = m_ref.shape[0]
    _softmax_init(m_ref, l_ref)
    acc_ref[...] = jnp.zeros(acc_ref.shape, F32)

    def tile_of(k):
        return jnp.where(k == 0, i, k - 1)

    def produce(slot, k, first):
        k0 = pl.multiple_of(tile_of(k) * t, t)
        for u in range(n_str):
            s = _dot_nt(q_ref[:, u * d:(u + 1) * d], k_ref[pl.ds(k0, t), u * d:(u + 1) * d])
            if first:
                causal = lax.broadcasted_iota(jnp.int32, (t, t), 1) <= lax.broadcasted_iota(jnp.int32, (t, t), 0)
                s = jnp.where(causal, s, NEG)
            s_ref[slot, u] = s

    def consume(slot, k):
        k0 = pl.multiple_of(tile_of(k) * t, t)
        for u in range(n_str):
            _softmax_rows(s_ref.at[slot, u], p_ref.at[u], al_ref.at[u], m_ref.at[u], l_ref.at[u], SOFTMAX_CHUNK)
        for u in range(n_str):
            vv = v_ref[pl.ds(k0, t), (u // 2) * 2 * d:(u // 2 + 1) * 2 * d]
            acc_ref[u] = acc_ref[u] * _lane_tile(al_ref[u], 2) + _dot(p_ref[u], vv)

    _two_slot_pipeline(i + 1, produce, consume)
    lam = (jnp.exp(jnp.sum(lq1_ref[...] * lk1_ref[...], axis=-1, keepdims=True))
           - jnp.exp(jnp.sum(lq2_ref[...] * lk2_ref[...], axis=-1, keepdims=True)) + lambda_init)
    for hh in range(n_str // 2):
        cols = slice(hh * 2 * d, (hh + 1) * 2 * d)
        inv = [1.0 / _lane_tile(l_ref[2 * hh + c], 2) for c in range(2)]
        o = acc_ref[2 * hh] * inv[0] - lam * (acc_ref[2 * hh + 1] * inv[1])
        o = o * lax.rsqrt(jnp.mean(o * o, axis=-1, keepdims=True) + RMS_EPS)
        o = o * gn_ref[hh:hh + 1, :] * (1.0 - lambda_init)
        o_ref[:, cols] = (o * _silu(gate_ref[:, cols].astype(F32))).astype(o_ref.dtype)


def _diff_attn(h, gn_g, lq1, lk1, lq2, lk2, lambda_init, batch, seq, t, hps):
    nh = DIFF_HEADS
    nq = seq // t
    w = hps * 2 * HEAD_DIM
    ng = nh // hps
    assert nh % hps == 0
    vec = pl.BlockSpec((1, HEAD_DIM), lambda b, h_, i: (0, 0))
    r = lambda a: a.reshape(1, HEAD_DIM)
    return pl.pallas_call(
        functools.partial(_diff_kernel, t=t, lambda_init=lambda_init),
        grid=(batch, ng, nq),
        in_specs=[
            pl.BlockSpec((t, w), lambda b, h_, i: (b * nq + i, h_)),
            pl.BlockSpec((seq, w), lambda b, h_, i: (b, ng + h_)),
            pl.BlockSpec((seq, w), lambda b, h_, i: (b, 2 * ng + h_)),
            pl.BlockSpec((t, w), lambda b, h_, i: (b * nq + i, 3 * ng + h_)),
            pl.BlockSpec((None, hps, 2 * HEAD_DIM), lambda b, h_, i: (h_, 0, 0)),
            vec, vec, vec, vec,
        ],
        out_specs=pl.BlockSpec((t, w), lambda b, h_, i: (b * nq + i, h_)),
        out_shape=jax.ShapeDtypeStruct((batch * seq, nh * 2 * HEAD_DIM), BF16),
        scratch_shapes=[
            pltpu.VMEM((2 * hps, t, LANES), F32), pltpu.VMEM((2 * hps, t, LANES), F32),
            pltpu.VMEM((2 * hps, t, 2 * HEAD_DIM), F32), pltpu.VMEM((2, 2 * hps, t, t), F32),
            pltpu.VMEM((2 * hps, t, t), BF16), pltpu.VMEM((2 * hps, t, LANES), F32),
        ],
        compiler_params=_cparams(("arbitrary", "arbitrary", "arbitrary")),
        name="diff_attn",
    )(h, h, h, h, gn_g.reshape(ng, hps, 2 * HEAD_DIM), r(lq1), r(lk1), r(lq2), r(lk2))


def _rope_tables(seq):
    pos = jnp.arange(seq, dtype=F32)
    inv = ROPE_THETA ** (-jnp.arange(0, HEAD_DIM, 2, dtype=F32) / HEAD_DIM)
    ang = pos[:, None] * inv[None, :]
    cos, sin = jnp.cos(ang), jnp.sin(ang)
    return jnp.concatenate([cos, cos], axis=-1), jnp.concatenate([-sin, sin], axis=-1)


def _even_weight(w_in):
    ga0 = EV_GATE_A * LANES
    n_ga = 3 * NSA_HEADS
    w_in = w_in.astype(BF16)
    pad = jnp.zeros((w_in.shape[0], EV_WIDTH - w_in.shape[1]), BF16)
    w = jnp.concatenate([w_in[:, :ga0], w_in[:, ga0 + n_ga:], w_in[:, ga0:ga0 + n_ga], pad], axis=1)
    assert w.shape[1] == EV_WIDTH and ga0 + (EV_GA - EV_GATE_A) * LANES + n_ga == w_in.shape[1]
    return w


def _even_tile_modes(tn):
    modes = [(False, False)] * (EV_WIDTH // LANES)
    for c in range(EV_QA, EV_QA + NSA_HEADS):
        modes[c] = (False, True)
    for c in list(range(EV_KS, EV_KS + NSA_KV_HEADS)) + list(range(EV_KW, EV_KW + NSA_KV_HEADS)):
        modes[c] = (True, False)
    for c in range(EV_QB, EV_QB + SB_HEADS):
        modes[c] = (False, True)
    per = tn // LANES
    return tuple(tuple(modes[i * per:(i + 1) * per]) for i in range(len(modes) // per))


def _odd_tile_modes(tn):
    nh = 2 * DIFF_HEADS
    modes = [(True, True)] * nh + [(True, False)] * nh + [(False, False)] * (2 * nh)
    per = tn // LANES
    return tuple(tuple(modes[i * per:(i + 1) * per]) for i in range(len(modes) // per))


TM_PROJ, TN_PROJ = 1024, 1024
TM_OUT = 512
TQ_NSA = 128
TK_NSA = 512
T_SB = 512
SB_HEADS_PER_STEP = 2
SB_TRI_BLOCK = 256
T_DIFF = 512
DIFF_HEADS_PER_STEP = 1


def _even_layer(x2, batch, seq, w_in, pe_k, pe_v, w1k, w2k, w1v, w2v, w_out, ln_g, ln_b, cos2, sin2):
    d = HEAD_DIM
    tm = min(TM_PROJ, seq)
    h = _inproj(x2, _even_weight(w_in), cos2, sin2, _even_tile_modes(TN_PROJ), tm, TN_PROJ)
    pe = jnp.stack([pe_k, pe_v])
    w1 = jnp.stack([w1k, w1v]).reshape(2, CMP_LEN, d, d).astype(BF16)
    w2 = jnp.stack([w2k, w2v]).astype(BF16)
    cmp_kv = _compress(h, pe, w1, w2, batch, seq)
    ocmp, sel = _cmp_topk(h, cmp_kv, batch, seq, TQ_NSA)
    o_a = _nsa_attn(h, sel, ocmp, cos2, sin2, batch, seq, TQ_NSA, min(TK_NSA, seq))
    o_b = _stick_breaking(h, batch, seq, min(T_SB, seq), SB_HEADS_PER_STEP, SB_TRI_BLOCK)
    return _outproj_ln([o_a, o_b], w_out.astype(BF16), x2, ln_g, ln_b, TM_OUT)


def _odd_layer(x2, batch, seq, w_in, lq1, lk1, lq2, lk2, gn_g, w_out, ln_g, ln_b, lambda_init, cos2, sin2):
    tm = min(TM_PROJ, seq)
    h = _inproj(x2, w_in.astype(BF16), cos2, sin2, _odd_tile_modes(TN_PROJ), tm, TN_PROJ)
    o = _diff_attn(h, gn_g, lq1, lk1, lq2, lk2, lambda_init, batch, seq, min(T_DIFF, seq), DIFF_HEADS_PER_STEP)
    return _outproj_ln([o], w_out.astype(BF16), x2, ln_g, ln_b, TM_OUT)


def kernel(x, ev_w_in, ev_pe_k, ev_pe_v, ev_w1_k, ev_w2_k, ev_w1_v, ev_w2_v, ev_w_out, ev_ln_g, ev_ln_b,
           od_w_in, od_lq1, od_lk1, od_lq2, od_lk2, od_gn_g, od_w_out, od_ln_g, od_ln_b):
    batch, seq, dm = x.shape
    cos2, sin2 = _rope_tables(seq)
    x2 = x.reshape(batch * seq, dm)
    for layer in range(DEPTH):
        i = layer // 2
        if layer % 2 == 0:
            x2 = _even_layer(x2, batch, seq, ev_w_in[i], ev_pe_k[i], ev_pe_v[i], ev_w1_k[i], ev_w2_k[i],
                             ev_w1_v[i], ev_w2_v[i], ev_w_out[i], ev_ln_g[i], ev_ln_b[i], cos2, sin2)
        else:
            lambda_init = 0.8 - 0.6 * math.exp(-0.3 * layer)
            x2 = _odd_layer(x2, batch, seq, od_w_in[i], od_lq1[i], od_lk1[i], od_lq2[i], od_lk2[i], od_gn_g[i],
                            od_w_out[i], od_ln_g[i], od_ln_b[i], lambda_init, cos2, sin2)
    return x2.reshape(batch, seq, dm)
```

```python
import functools
import math

import jax
import jax.numpy as jnp
from jax import lax
from jax.experimental import pallas as pl
from jax.experimental.pallas import tpu as pltpu

F32 = jnp.float32
BF16 = jnp.bfloat16

HEAD_DIM = 128
ROPE_THETA = 10000.0
NSA_HEADS = 8
NSA_KV_HEADS = 2
NSA_GROUP = NSA_HEADS // NSA_KV_HEADS
CMP_LEN = 32
CMP_STRIDE = 16
SEL_LEN = 64
SEL_SHIFT = SEL_LEN.bit_length() - 1
SEL_TOPK = 16
WIN = 512
SEL_FORCE = 1.0e4
SB_HEADS = 8
DIFF_HEADS = 8
DEPTH = 2
DEEPNORM_ALPHA = (2.0 * DEPTH) ** 0.25
LN_EPS = 1e-5
RMS_EPS = 1e-5
LOG2E = math.log2(math.e)
SCALE = HEAD_DIM ** -0.5 * LOG2E

LANES = 128
NEG = -1.0e30
OUT_ROW_CHUNK = 256
SOFTMAX_CHUNK = 64
VMEM_LIMIT = 56 * 1024 * 1024

MXU_TILE = 256
EV_QA, EV_KC, EV_VC, EV_KS, EV_VS = 0, 8, 10, 12, 14
EV_GATE_A, EV_QB, EV_KB, EV_VB, EV_GATE_B = 16, 24, 32, 40, 48
EV_KW, EV_VW, EV_GA = 56, 58, 60
EV_WIDTH = 64 * LANES


def _cparams(sem):
    return pltpu.CompilerParams(dimension_semantics=sem, vmem_limit_bytes=VMEM_LIMIT)


def _dot_nt(a, b):
    return lax.dot_general(a, b, (((1,), (1,)), ((), ())), preferred_element_type=F32)


def _dot(a, b):
    return jnp.dot(a, b, preferred_element_type=F32)


def _silu(x):
    return x * (1.0 / (1.0 + jnp.exp(-x)))


def _sigmoid(x):
    return 1.0 / (1.0 + jnp.exp(-x))


def _rope_f32(x, cos2, sin2):
    return x * cos2 + pltpu.roll(x, HEAD_DIM // 2, axis=1) * sin2


def _inproj_kernel(*refs, tile_modes, tile_src):
    n_w = len(refs) - 5
    x_ref, w_refs, (cos_ref, sin_ref, o_ref, xb_ref) = refs[0], refs[1:1 + n_w], refs[1 + n_w:]
    n = pl.program_id(1)

    @pl.when(n == 0)
    def _():
        xb_ref[...] = x_ref[...].astype(BF16)

    variants = {}
    for tile, pat in enumerate(tile_modes):
        variants.setdefault((tile_src[tile][0], pat), []).append(tile)
    per_chunk = MXU_TILE // LANES
    for (w_idx, pat), tiles in variants.items():
        cond = functools.reduce(jnp.logical_or, [n == t for t in tiles])

        @pl.when(cond)
        def _(pat=pat, w_ref=w_refs[w_idx]):
            for cc in range(len(pat) // per_chunk):
                acc = _dot(xb_ref[...], w_ref[:, cc * MXU_TILE:(cc + 1) * MXU_TILE])
                for c2 in range(per_chunk):
                    c = cc * per_chunk + c2
                    rope, scale = pat[c]
                    y = acc[:, c2 * LANES:(c2 + 1) * LANES]
                    if rope:
                        y = _rope_f32(y, cos_ref[...], sin_ref[...])
                    if scale:
                        y = y * SCALE
                    o_ref[:, c * LANES:(c + 1) * LANES] = y.astype(o_ref.dtype)


def _inproj(x, ws, tile_src, cos2, sin2, tile_modes, tm, tn):
    m, k = x.shape
    n_tiles = len(tile_src)
    s = cos2.shape[0]
    assert m % tm == 0 and s % tm == 0 and len(tile_modes) == n_tiles
    pos_tiles = s // tm

    def w_spec(a):
        used = [blk for (src, blk) in tile_src if src == a]
        table, last = [], used[0]
        for src, blk in tile_src:
            last = blk if src == a else last
            table.append(last)

        def index_map(i, j):
            blk = jnp.int32(table[0])
            for t in range(1, n_tiles):
                blk = jnp.where(j == t, jnp.int32(table[t]), blk)
            return (0, blk)

        return pl.BlockSpec((k, tn), index_map)

    return pl.pallas_call(
        functools.partial(_inproj_kernel, tile_modes=tile_modes, tile_src=tuple(tile_src)),
        grid=(m // tm, n_tiles),
        in_specs=[pl.BlockSpec((tm, k), lambda i, j: (i, 0))] + [w_spec(a) for a in range(len(ws))] + [
            pl.BlockSpec((tm, LANES), lambda i, j: (i % pos_tiles, 0)),
            pl.BlockSpec((tm, LANES), lambda i, j: (i % pos_tiles, 0)),
        ],
        out_specs=pl.BlockSpec((tm, tn), lambda i, j: (i, j)),
        out_shape=jax.ShapeDtypeStruct((m, n_tiles * tn), BF16),
        scratch_shapes=[pltpu.VMEM((tm, k), BF16)],
        compiler_params=_cparams(("arbitrary", "arbitrary")),
        name="inproj",
    )(x, *ws, cos2, sin2)


def _outproj_ln_kernel(*refs, n_parts):
    o_refs = refs[:n_parts]
    w_ref, x_ref, g_ref, b_ref, out_ref = refs[n_parts:]
    kp = w_ref.shape[0] // n_parts
    for r in range(x_ref.shape[0] // OUT_ROW_CHUNK):
        rows = slice(r * OUT_ROW_CHUNK, (r + 1) * OUT_ROW_CHUNK)
        y = None
        for p in range(n_parts):
            part = _dot(o_refs[p][rows, :], w_ref[p * kp:(p + 1) * kp, :])
            y = part if y is None else y + part
        z = DEEPNORM_ALPHA * x_ref[rows, :] + y
        mu = jnp.mean(z, axis=-1, keepdims=True)
        zc = z - mu
        var = jnp.mean(zc * zc, axis=-1, keepdims=True)
        out_ref[rows, :] = zc * lax.rsqrt(var + LN_EPS) * g_ref[...] + b_ref[...]


def _outproj_ln(parts, w, x, g, b, tm):
    m, d = x.shape
    n_parts = len(parts)
    kp = w.shape[0] // n_parts
    return pl.pallas_call(
        functools.partial(_outproj_ln_kernel, n_parts=n_parts),
        grid=(m // tm,),
        in_specs=[pl.BlockSpec((tm, kp), lambda i: (i, 0)) for _ in parts] + [
            pl.BlockSpec(w.shape, lambda i: (0, 0)),
            pl.BlockSpec((tm, d), lambda i: (i, 0)),
            pl.BlockSpec((1, d), lambda i: (0, 0)),
            pl.BlockSpec((1, d), lambda i: (0, 0)),
        ],
        out_specs=pl.BlockSpec((tm, d), lambda i: (i, 0)),
        out_shape=jax.ShapeDtypeStruct((m, d), F32),
        compiler_params=_cparams(("arbitrary",)),
        name="outproj_ln",
    )(*parts, w, x, g.reshape(1, d), b.reshape(1, d))


def _compress_kernel(kv_ref, pe_ref, w1_ref, w2_ref, o_ref, pad_ref, *, seq):
    n_rows = seq // CMP_STRIDE
    pad_ref[0:seq, :] = kv_ref[...].astype(F32)
    pad_ref[seq:seq + CMP_STRIDE, :] = jnp.zeros((CMP_STRIDE, HEAD_DIM), F32)
    pre = jnp.zeros((n_rows, HEAD_DIM), F32)
    for l in range(CMP_LEN):
        rows = pad_ref[pl.ds(l, n_rows, stride=CMP_STRIDE), :] + pe_ref[l:l + 1, :]
        pre = pre + _dot(rows.astype(BF16), w1_ref[l])
    o_ref[...] = _dot(_silu(pre).astype(BF16), w2_ref[...]).astype(o_ref.dtype)


def _compress(h, pe, w1, w2, batch, seq):
    n_rows = seq // CMP_STRIDE
    hk = NSA_KV_HEADS
    return pl.pallas_call(
        functools.partial(_compress_kernel, seq=seq),
        grid=(2, batch, hk),
        in_specs=[
            pl.BlockSpec((seq, HEAD_DIM), lambda c, b, h: (b, EV_KC + c * hk + h)),
            pl.BlockSpec((None, CMP_LEN, HEAD_DIM), lambda c, b, h: (c, 0, 0)),
            pl.BlockSpec((None, CMP_LEN, HEAD_DIM, HEAD_DIM), lambda c, b, h: (c, 0, 0, 0)),
            pl.BlockSpec((None, HEAD_DIM, HEAD_DIM), lambda c, b, h: (c, 0, 0)),
        ],
        out_specs=pl.BlockSpec((None, None, n_rows, HEAD_DIM), lambda c, b, h: (c, b * hk + h, 0, 0)),
        out_shape=jax.ShapeDtypeStruct((2, batch * hk, n_rows, HEAD_DIM), BF16),
        scratch_shapes=[pltpu.VMEM((seq + CMP_STRIDE, HEAD_DIM), F32)],
        compiler_params=_cparams(("arbitrary", "arbitrary", "arbitrary")),
        name="nsa_compress",
    )(h, pe, w1, w2)


def _heads_to_rows(x, g):
    return jnp.concatenate([x[:, i * HEAD_DIM:(i + 1) * HEAD_DIM] for i in range(g)], axis=0)


def _rows_to_heads(x, g):
    r = x.shape[0] // g
    return jnp.concatenate([x[i * r:(i + 1) * r, :] for i in range(g)], axis=1)


def _cmp_topk_kernel(q_ref, kc_ref, vc_ref, ocmp_ref, sel_ref, *, tq, n_sel_blk, n_sel):
    g, hk = NSA_GROUP, NSA_KV_HEADS
    i = pl.program_id(1)
    q0 = i * tq
    n_rows = kc_ref.shape[1]
    t_col = q0 + lax.broadcasted_iota(jnp.int32, (tq, n_rows), 0)
    blk_end = lax.broadcasted_iota(jnp.int32, (tq, n_rows), 1) * CMP_STRIDE + (CMP_LEN - 1)
    vis = (blk_end <= t_col)[None]
    jj = lax.broadcasted_iota(jnp.int32, (n_sel_blk, n_rows), 0) * SEL_LEN
    cs = lax.broadcasted_iota(jnp.int32, (n_sel_blk, n_rows), 1) * CMP_STRIDE
    ov_t = jnp.where((cs < jj + SEL_LEN) & (cs + CMP_LEN > jj), 1.0, 0.0).astype(BF16)
    blk = lax.broadcasted_iota(jnp.int32, (n_sel_blk, tq), 0)
    j_cur = (q0 + lax.broadcasted_iota(jnp.int32, (n_sel_blk, tq), 1)) >> SEL_SHIFT
    forced = (blk == 0) | (blk == j_cur) | (blk == j_cur - 1)

    scores = []
    for h in range(hk):
        q4 = _heads_to_rows(q_ref[:, h * g * HEAD_DIM:(h + 1) * g * HEAD_DIM], g)
        sc = _dot_nt(q4, kc_ref[h]).reshape(g, tq, n_rows)
        sm = jnp.where(vis, sc, NEG)
        mx = jnp.max(sm, axis=-1, keepdims=True)
        e = jnp.where(vis, jnp.exp2(sm - mx), 0.0)
        den = jnp.sum(e, axis=-1, keepdims=True)
        p = e / jnp.where(den > 0.0, den, 1.0)
        o = _dot(p.reshape(g * tq, n_rows).astype(BF16), vc_ref[h])
        ocmp_ref[:, h * g * HEAD_DIM:(h + 1) * g * HEAD_DIM] = _rows_to_heads(o, g).astype(ocmp_ref.dtype)

        p_sum = jnp.sum(p, axis=0)
        p_hi = p_sum.astype(BF16)
        p_lo = (p_sum - p_hi.astype(F32)).astype(BF16)
        imp_t = _dot_nt(ov_t, p_hi) + _dot_nt(ov_t, p_lo)
        scores.append(jnp.where(forced, SEL_FORCE, jnp.where(blk <= j_cur, imp_t, -SEL_FORCE)))

    ranks = [jnp.zeros((n_sel_blk, tq), F32) for _ in range(hk)]
    for r in range(n_sel_blk):
        for h in range(hk):
            row = scores[h][r:r + 1, :]
            ahead = (row > scores[h]) | ((row == scores[h]) & (blk > r))
            ranks[h] = ranks[h] + jnp.where(ahead, 1.0, 0.0)
    for h in range(hk):
        sel_t = jnp.where(ranks[h] < float(n_sel), 1.0, 0.0)
        if n_sel_blk < LANES:
            sel_t = jnp.concatenate([sel_t, jnp.zeros((LANES - n_sel_blk, tq), F32)], axis=0)
        sel_ref[h] = sel_t.T.astype(sel_ref.dtype)


def _cmp_topk(h, cmp_kv, batch, seq, tq):
    hk = NSA_KV_HEADS
    nq = seq // tq
    n_rows = seq // CMP_STRIDE
    n_sel_blk = seq // SEL_LEN
    wq = NSA_HEADS * HEAD_DIM
    assert n_sel_blk <= LANES and tq == LANES
    n_sel = min(SEL_TOPK, n_sel_blk)
    ocmp, sel = pl.pallas_call(
        functools.partial(_cmp_topk_kernel, tq=tq, n_sel_blk=n_sel_blk, n_sel=n_sel),
        grid=(batch, nq),
        in_specs=[
            pl.BlockSpec((tq, wq), lambda b, i: (b * nq + i, EV_QA * LANES // wq)),
            pl.BlockSpec((None, hk, n_rows, HEAD_DIM), lambda b, i: (0, b, 0, 0)),
            pl.BlockSpec((None, hk, n_rows, HEAD_DIM), lambda b, i: (1, b, 0, 0)),
        ],
        out_specs=[
            pl.BlockSpec((tq, wq), lambda b, i: (b * nq + i, 0)),
            pl.BlockSpec((hk, tq, LANES), lambda b, i: (b, i, 0)),
        ],
        out_shape=[
            jax.ShapeDtypeStruct((batch * seq, wq), BF16),
            jax.ShapeDtypeStruct((batch * hk, seq, LANES), BF16),
        ],
        compiler_params=_cparams(("arbitrary", "arbitrary")),
        name="nsa_cmp_topk",
    )(h, cmp_kv, cmp_kv)
    return ocmp, sel.reshape(batch * hk * seq, LANES)


def _lane_tile(x, n):
    return x if n == 1 else jnp.concatenate([x] * n, axis=1)


def _softmax_init(m_ref, l_ref):
    m_ref[...] = jnp.full(m_ref.shape, NEG, F32)
    l_ref[...] = jnp.zeros(l_ref.shape, F32)


def _softmax_rows(s_ref, p_ref, al_ref, m_ref, l_ref, chunk):
    rows, tk = s_ref.shape
    for r in range(rows // chunk):
        sl = slice(r * chunk, (r + 1) * chunk)
        s = s_ref[sl, :]
        m_prev = m_ref[sl, :]
        m_new = jnp.maximum(m_prev, jnp.max(s, axis=1, keepdims=True))
        alpha = jnp.exp2(m_prev - m_new)
        p = jnp.exp2(s - _lane_tile(m_new, tk // LANES))
        l_ref[sl, :] = alpha * l_ref[sl, :] + jnp.sum(p, axis=1, keepdims=True)
        m_ref[sl, :] = m_new
        al_ref[sl, :] = alpha
        p_ref[sl, :] = p.astype(BF16)


def _two_slot_pipeline(n_steps, produce, consume):
    produce(0, 0, True)

    def body(k, _):
        for parity in range(2):
            @pl.when(k % 2 == parity)
            def _(parity=parity):
                produce(1 - parity, k + 1, False)
                consume(parity, k)
        return 0

    lax.fori_loop(0, n_steps - 1, body, 0)
    last = n_steps - 1
    for parity in range(2):
        @pl.when(last % 2 == parity)
        def _(parity=parity):
            consume(parity, last)


def _carried_pipeline(i, produce_first, produce, consume):
    before = (i * (i + 1) // 2) % 2

    @pl.when(i == 0)
    def _():
        produce_first(0, i)

    def body(k, _):
        for parity in range(2):
            @pl.when((before + k) % 2 == parity)
            def _(parity=parity):
                produce(1 - parity, k + 1)
                consume(parity, k)
        return 0

    lax.fori_loop(0, i, body, 0)
    for parity in range(2):
        @pl.when((before + i) % 2 == parity)
        def _(parity=parity):
            produce_first(1 - parity, i + 1)
            consume(parity, i)


def _nsa_attn_kernel(q_ref, sel0_ref, sel1_ref, ocmp_ref, ks_ref, vs_ref, kw_ref, vw_ref, cos_ref, sin_ref, ga_ref,
                     gate0_ref, gate1_ref, o_ref, m_ref, l_ref, al_ref, acc_ref, q4_ref, s_ref, p_ref, sw_ref,
                     pw_ref, ow_ref, *, tq, tk):
    g, hk = NSA_GROUP, NSA_KV_HEADS
    rows = g * tq
    i = pl.program_id(1)
    q0 = i * tq
    sel_refs = (sel0_ref, sel1_ref)
    gate_refs = (gate0_ref, gate1_ref)
    kv_cols = lambda h: slice(h * HEAD_DIM, (h + 1) * HEAD_DIM)
    cos2 = jnp.concatenate([cos_ref[...]] * g, axis=0)
    sin2 = jnp.concatenate([sin_ref[...]] * g, axis=0)
    for h in range(hk):
        q_h = q_ref[:, h * g * HEAD_DIM:(h + 1) * g * HEAD_DIM]
        q4_ref[h] = _rope_f32(_heads_to_rows(q_h, g).astype(F32), cos2, sin2).astype(BF16)

    span = WIN + tq
    w0 = pl.multiple_of(jnp.maximum(q0 - WIN, 0), tq)
    dist = (q0 + lax.broadcasted_iota(jnp.int32, (tq, span), 0)) - (w0 + lax.broadcasted_iota(jnp.int32, (tq, span), 1))
    in_window = (dist >= 0) & (dist < WIN)
    _softmax_init(m_ref, l_ref)
    for h in range(hk):
        s = _dot_nt(q4_ref[h], kw_ref[pl.ds(w0, span), kv_cols(h)])
        sw_ref[h] = jnp.where(in_window[None], s.reshape(g, tq, span), NEG).reshape(rows, span)
    for h in range(hk):
        _softmax_rows(sw_ref.at[h], pw_ref.at[h], al_ref.at[h], m_ref.at[h], l_ref.at[h], SOFTMAX_CHUNK)
    for h in range(hk):
        ow_ref[h] = _dot(pw_ref[h], vw_ref[pl.ds(w0, span), kv_cols(h)]) / l_ref[h]

    _softmax_init(m_ref, l_ref)
    acc_ref[...] = jnp.zeros(acc_ref.shape, F32)

    def produce(slot, j, first):
        del first
        k0 = pl.multiple_of(j * tk, tk)
        blk_row = lax.broadcasted_iota(jnp.int32, (LANES, tk), 0)
        blk_of_lane = lax.broadcasted_iota(jnp.int32, (LANES, tk), 1) >> SEL_SHIFT
        expand = jnp.where(blk_row == j * (tk // SEL_LEN) + blk_of_lane, 1.0, 0.0).astype(BF16)
        causal = k0 + lax.broadcasted_iota(jnp.int32, (tq, tk), 1) <= q0 + lax.broadcasted_iota(jnp.int32, (tq, tk), 0)
        for h in range(hk):
            s = _dot_nt(q4_ref[h], ks_ref[pl.ds(k0, tk), kv_cols(h)])
            picked = _dot(sel_refs[h][...], expand)
            valid = (picked > 0.5) & causal
            s_ref[slot, h] = jnp.where(valid[None], s.reshape(g, tq, tk), NEG).reshape(rows, tk)

    def consume(slot, j):
        k0 = pl.multiple_of(j * tk, tk)
        for h in range(hk):
            _softmax_rows(s_ref.at[slot, h], p_ref.at[h], al_ref.at[h], m_ref.at[h], l_ref.at[h], SOFTMAX_CHUNK)
        for h in range(hk):
            acc_ref[h] = acc_ref[h] * al_ref[h] + _dot(p_ref[h], vs_ref[pl.ds(k0, tk), kv_cols(h)])

    _two_slot_pipeline((q0 + tq - 1) // tk + 1, produce, consume)

    gt = _sigmoid(ga_ref[...].astype(F32))
    for h in range(hk):
        for a in range(g):
            cols = slice((h * g + a) * HEAD_DIM, (h * g + a + 1) * HEAD_DIM)
            head = slice(a * tq, (a + 1) * tq)
            lane = (h * g + a) * 3
            o_sel = acc_ref[h, head, :] / l_ref[h, head, :]
            mix = (gt[:, lane:lane + 1] * ocmp_ref[:, cols].astype(F32) + gt[:, lane + 1:lane + 2] * o_sel
                   + gt[:, lane + 2:lane + 3] * ow_ref[h, head, :])
            gate = _silu(gate_refs[h][:, a * HEAD_DIM:(a + 1) * HEAD_DIM].astype(F32))
            o_ref[:, cols] = (mix * gate).astype(o_ref.dtype)


def _nsa_attn(h, sel, ocmp, cos2, sin2, batch, seq, tq, tk):
    hk, g = NSA_KV_HEADS, NSA_GROUP
    nq = seq // tq
    rows = g * tq
    wq = NSA_HEADS * HEAD_DIM
    wkv = hk * HEAD_DIM
    assert WIN % tq == 0 and tk % SEL_LEN == 0 and seq % tk == 0 and seq >= WIN + tq and tk % tq == 0 and hk == 2
    slab = lambda col: pl.BlockSpec((seq, wkv), lambda b, i: (b, col // hk))
    sel_spec = lambda hh: pl.BlockSpec((tq, LANES), lambda b, i: ((b * hk + hh) * nq + i, 0))
    return pl.pallas_call(
        functools.partial(_nsa_attn_kernel, tq=tq, tk=tk),
        grid=(batch, nq),
        in_specs=[
            pl.BlockSpec((tq, wq), lambda b, i: (b * nq + i, EV_QA * LANES // wq)),
            sel_spec(0), sel_spec(1),
            pl.BlockSpec((tq, wq), lambda b, i: (b * nq + i, 0)),
            slab(EV_KS), slab(EV_VS), slab(EV_KW), slab(EV_VW),
            pl.BlockSpec((tq, LANES), lambda b, i: (i, 0)),
            pl.BlockSpec((tq, LANES), lambda b, i: (i, 0)),
            pl.BlockSpec((tq, LANES), lambda b, i: (b * nq + i, EV_GA)),
            pl.BlockSpec((tq, g * HEAD_DIM), lambda b, i: (b * nq + i, EV_GATE_A // g)),
            pl.BlockSpec((tq, g * HEAD_DIM), lambda b, i: (b * nq + i, EV_GATE_A // g + 1)),
        ],
        out_specs=pl.BlockSpec((tq, wq), lambda b, i: (b * nq + i, 0)),
        out_shape=jax.ShapeDtypeStruct((batch * seq, wq), BF16),
        scratch_shapes=[
            pltpu.VMEM((hk, rows, LANES), F32), pltpu.VMEM((hk, rows, LANES), F32), pltpu.VMEM((hk, rows, LANES), F32),
            pltpu.VMEM((hk, rows, HEAD_DIM), F32), pltpu.VMEM((hk, rows, HEAD_DIM), BF16),
            pltpu.VMEM((2, hk, rows, tk), F32), pltpu.VMEM((hk, rows, tk), BF16),
            pltpu.VMEM((hk, rows, WIN + tq), F32), pltpu.VMEM((hk, rows, WIN + tq), BF16),
            pltpu.VMEM((hk, rows, HEAD_DIM), F32),
        ],
        compiler_params=_cparams(("arbitrary", "arbitrary")),
        name="nsa_attn",
    )(h, sel, sel, ocmp, h, h, h, h, cos2, sin2, h, h, h)


def _sb_kernel(q_ref, k_ref, v_ref, gate_ref, tri_ref, o_ref, later_ref, acc_ref, z_ref, zs_ref, sp16_ref, sp0_ref,
               after_ref, a_ref, *, t):
    i = pl.program_id(2)
    n_q = pl.num_programs(2)
    chunk = SOFTMAX_CHUNK
    nh = acc_ref.shape[0]
    tb = tri_ref.shape[0]
    nb = t // tb
    later_ref[...] = jnp.zeros(later_ref.shape, F32)
    acc_ref[...] = jnp.zeros(acc_ref.shape, F32)
    head_cols = lambda h: slice(h * HEAD_DIM, (h + 1) * HEAD_DIM)

    def logits(slot, qi, kj, diagonal):
        q0 = pl.multiple_of(qi * t, t)
        k0 = pl.multiple_of(kj * t, t)
        for h in range(nh):
            z = _dot_nt(q_ref[pl.ds(q0, t), head_cols(h)], k_ref[pl.ds(k0, t), head_cols(h)])
            if diagonal:
                strict = lax.broadcasted_iota(jnp.int32, (t, t), 1) < lax.broadcasted_iota(jnp.int32, (t, t), 0)
                z = jnp.where(strict, z, NEG)
            z_ref[slot, h] = z

    def produce_first(slot, qi):
        qi = jnp.minimum(qi, n_q - 1)
        logits(slot, qi, qi, True)

    def produce(slot, k):
        logits(slot, i, i - k, False)

    def consume(slot, k):
        k0 = pl.multiple_of((i - k) * t, t)
        for h in range(nh):
            for r in range(t // chunk):
                sl = slice(r * chunk, (r + 1) * chunk)
                z = z_ref[slot, h, sl, :]
                sp = jnp.maximum(z, 0.0) + jnp.log(1.0 + jnp.exp2(jnp.minimum(z, -z))) * LOG2E
                zs_ref[h, sl, :] = z - sp
                sp16_ref[h, sl, :] = sp.astype(BF16)
                sp0_ref[h, sl, :] = jnp.concatenate([sp[:, b * tb:b * tb + LANES] for b in range(nb)], axis=1)
        for h in range(nh):
            for b in range(nb):
                bc = slice(b * tb, (b + 1) * tb)
                after_ref[h, :, bc] = _dot(sp16_ref[h, :, bc], tri_ref[...])
        for h in range(nh):
            for r in range(t // chunk):
                sl = slice(r * chunk, (r + 1) * chunk)
                run = later_ref[h, sl, :]
                for b in reversed(range(nb)):
                    bc = slice(b * tb, (b + 1) * tb)
                    after = after_ref[h, sl, bc]
                    a_ref[h, sl, bc] = jnp.exp2(zs_ref[h, sl, bc] - after - _lane_tile(run, tb // LANES)).astype(BF16)
                    run = run + (after[:, 0:1] + sp0_ref[h, sl, b * LANES:b * LANES + 1])
                later_ref[h, sl, :] = run
            acc_ref[h] += _dot(a_ref[h], v_ref[pl.ds(k0, t), head_cols(h)])

    _carried_pipeline(i, produce_first, produce, consume)
    for h in range(nh):
        o_ref[:, head_cols(h)] = (acc_ref[h] * _silu(gate_ref[:, head_cols(h)].astype(F32))).astype(o_ref.dtype)


def _stick_breaking(h, batch, seq, t, nh, tb):
    nq = seq // t
    w = nh * HEAD_DIM
    assert SB_HEADS % nh == 0 and t % tb == 0
    tri = (lax.broadcasted_iota(jnp.int32, (tb, tb), 0) > lax.broadcasted_iota(jnp.int32, (tb, tb), 1)).astype(BF16)
    col = lambda first_block: first_block // nh
    return pl.pallas_call(
        functools.partial(_sb_kernel, t=t),
        grid=(batch, SB_HEADS // nh, nq),
        in_specs=[
            pl.BlockSpec((seq, w), lambda b, h_, i: (b, col(EV_QB) + h_)),
            pl.BlockSpec((seq, w), lambda b, h_, i: (b, col(EV_KB) + h_)),
            pl.BlockSpec((seq, w), lambda b, h_, i: (b, col(EV_VB) + h_)),
            pl.BlockSpec((t, w), lambda b, h_, i: (b * nq + i, col(EV_GATE_B) + h_)),
            pl.BlockSpec((tb, tb), lambda b, h_, i: (0, 0)),
        ],
        out_specs=pl.BlockSpec((t, w), lambda b, h_, i: (b * nq + i, h_)),
        out_shape=jax.ShapeDtypeStruct((batch * seq, SB_HEADS * HEAD_DIM), BF16),
        scratch_shapes=[
            pltpu.VMEM((nh, t, LANES), F32), pltpu.VMEM((nh, t, HEAD_DIM), F32), pltpu.VMEM((2, nh, t, t), F32),
            pltpu.VMEM((nh, t, t), F32), pltpu.VMEM((nh, t, t), BF16), pltpu.VMEM((nh, t, (t // tb) * LANES), F32),
            pltpu.VMEM((nh, t, t), F32), pltpu.VMEM((nh, t, t), BF16),
        ],
        compiler_params=_cparams(("arbitrary", "arbitrary", "arbitrary")),
        name="stick_breaking",
    )(h, h, h, h, tri)


def _diff_kernel(q_ref, k_ref, v_ref, gate_ref, gn_ref, lq1_ref, lk1_ref, lq2_ref, lk2_ref, o_ref,
                 m_ref, l_ref, acc_ref, s_ref, p_ref, al_ref, *, t, lambda_init):
    d = HEAD_DIM
    i = pl.program_id(2)
    n_q = pl.num_programs(2)
    n_str = m_ref.shape[0]
    _softmax_init(m_ref, l_ref)
    acc_ref[...] = jnp.zeros(acc_ref.shape, F32)

    def scores(slot, qi, kj, diagonal):
        q0 = pl.multiple_of(qi * t, t)
        k0 = pl.multiple_of(kj * t, t)
        for u in range(n_str):
            s = _dot_nt(q_ref[pl.ds(q0, t), u * d:(u + 1) * d], k_ref[pl.ds(k0, t), u * d:(u + 1) * d])
            if diagonal:
                causal = lax.broadcasted_iota(jnp.int32, (t, t), 1) <= lax.broadcasted_iota(jnp.int32, (t, t), 0)
                s = jnp.where(causal, s, NEG)
            s_ref[slot, u] = s

    def produce_first(slot, qi):
        qi = jnp.minimum(qi, n_q - 1)
        scores(slot, qi, qi, True)

    def produce(slot, k):
        scores(slot, i, k - 1, False)

    def consume(slot, k):
        k0 = pl.multiple_of(jnp.where(k == 0, i, k - 1) * t, t)
        for u in range(n_str):
            _softmax_rows(s_ref.at[slot, u], p_ref.at[u], al_ref.at[u], m_ref.at[u], l_ref.at[u], SOFTMAX_CHUNK)
        for u in range(n_str):
            vv = v_ref[pl.ds(k0, t), (u // 2) * 2 * d:(u // 2 + 1) * 2 * d]
            acc_ref[u] = acc_ref[u] * _lane_tile(al_ref[u], 2) + _dot(p_ref[u], vv)

    _carried_pipeline(i, produce_first, produce, consume)
    lam = (jnp.exp(jnp.sum(lq1_ref[...] * lk1_ref[...], axis=-1, keepdims=True))
           - jnp.exp(jnp.sum(lq2_ref[...] * lk2_ref[...], axis=-1, keepdims=True)) + lambda_init)
    for hh in range(n_str // 2):
        cols = slice(hh * 2 * d, (hh + 1) * 2 * d)
        inv = [1.0 / _lane_tile(l_ref[2 * hh + c], 2) for c in range(2)]
        o = acc_ref[2 * hh] * inv[0] - lam * (acc_ref[2 * hh + 1] * inv[1])
        o = o * lax.rsqrt(jnp.mean(o * o, axis=-1, keepdims=True) + RMS_EPS)
        o = o * gn_ref[hh:hh + 1, :] * (1.0 - lambda_init)
        o_ref[:, cols] = (o * _silu(gate_ref[:, cols].astype(F32))).astype(o_ref.dtype)


def _diff_attn(h, gn_g, lq1, lk1, lq2, lk2, lambda_init, batch, seq, t, hps):
    nh = DIFF_HEADS
    nq = seq // t
    w = hps * 2 * HEAD_DIM
    ng = nh // hps
    assert nh % hps == 0
    vec = pl.BlockSpec((1, HEAD_DIM), lambda b, h_, i: (0, 0))
    r = lambda a: a.reshape(1, HEAD_DIM)
    return pl.pallas_call(
        functools.partial(_diff_kernel, t=t, lambda_init=lambda_init),
        grid=(batch, ng, nq),
        in_specs=[
            pl.BlockSpec((seq, w), lambda b, h_, i: (b, h_)),
            pl.BlockSpec((seq, w), lambda b, h_, i: (b, ng + h_)),
            pl.BlockSpec((seq, w), lambda b, h_, i: (b, 2 * ng + h_)),
            pl.BlockSpec((t, w), lambda b, h_, i: (b * nq + i, 3 * ng + h_)),
            pl.BlockSpec((None, hps, 2 * HEAD_DIM), lambda b, h_, i: (h_, 0, 0)),
            vec, vec, vec, vec,
        ],
        out_specs=pl.BlockSpec((t, w), lambda b, h_, i: (b * nq + i, h_)),
        out_shape=jax.ShapeDtypeStruct((batch * seq, nh * 2 * HEAD_DIM), BF16),
        scratch_shapes=[
            pltpu.VMEM((2 * hps, t, LANES), F32), pltpu.VMEM((2 * hps, t, LANES), F32),
            pltpu.VMEM((2 * hps, t, 2 * HEAD_DIM), F32), pltpu.VMEM((2, 2 * hps, t, t), F32),
            pltpu.VMEM((2 * hps, t, t), BF16), pltpu.VMEM((2 * hps, t, LANES), F32),
        ],
        compiler_params=_cparams(("arbitrary", "arbitrary", "arbitrary")),
        name="diff_attn",
    )(h, h, h, h, gn_g.reshape(ng, hps, 2 * HEAD_DIM), r(lq1), r(lk1), r(lq2), r(lk2))


def _rope_tables(seq):
    pos = jnp.arange(seq, dtype=F32)
    inv = ROPE_THETA ** (-jnp.arange(0, HEAD_DIM, 2, dtype=F32) / HEAD_DIM)
    ang = pos[:, None] * inv[None, :]
    cos, sin = jnp.cos(ang), jnp.sin(ang)
    return jnp.concatenate([cos, cos], axis=-1), jnp.concatenate([-sin, sin], axis=-1)


def _even_weights(w_in, tn):
    kw0 = EV_GATE_A * LANES
    ga0 = kw0 + 2 * NSA_KV_HEADS * HEAD_DIM
    n_ga = 3 * NSA_HEADS
    tail = w_in[:, ga0 + n_ga:]
    assert kw0 % tn == 0 and tail.shape[1] % tn == 0 and (EV_KW - EV_GATE_A) * LANES == tail.shape[1]
    pad = jnp.zeros((w_in.shape[0], tn - (ga0 - kw0) - n_ga), w_in.dtype)
    w_a = jnp.concatenate([w_in[:, :kw0], w_in[:, kw0:ga0], w_in[:, ga0:ga0 + n_ga], pad], axis=1).astype(BF16)
    w_b = tail.astype(BF16)
    n_a, n_b = kw0 // tn, tail.shape[1] // tn
    tile_src = [(0, t) for t in range(n_a)] + [(1, t) for t in range(n_b)] + [(0, n_a)]
    assert len(tile_src) * tn == EV_WIDTH
    return [w_a, w_b], tile_src


def _even_tile_modes(tn):
    modes = [(False, False)] * (EV_WIDTH // LANES)
    for c in range(EV_QA, EV_QA + NSA_HEADS):
        modes[c] = (False, True)
    for c in list(range(EV_KS, EV_KS + NSA_KV_HEADS)) + list(range(EV_KW, EV_KW + NSA_KV_HEADS)):
        modes[c] = (True, False)
    for c in range(EV_QB, EV_QB + SB_HEADS):
        modes[c] = (False, True)
    per = tn // LANES
    return tuple(tuple(modes[i * per:(i + 1) * per]) for i in range(len(modes) // per))


def _odd_tile_modes(tn):
    nh = 2 * DIFF_HEADS
    modes = [(True, True)] * nh + [(True, False)] * nh + [(False, False)] * (2 * nh)
    per = tn // LANES
    return tuple(tuple(modes[i * per:(i + 1) * per]) for i in range(len(modes) // per))


TM_PROJ, TN_PROJ = 1024, 1024
TM_OUT = 512
TQ_NSA = 128
TK_NSA = 512
T_SB = 512
SB_HEADS_PER_STEP = 2
SB_TRI_BLOCK = 256
T_DIFF = 512
DIFF_HEADS_PER_STEP = 1


def _even_layer(x2, batch, seq, w_in, pe_k, pe_v, w1k, w2k, w1v, w2v, w_out, ln_g, ln_b, cos2, sin2):
    d = HEAD_DIM
    tm = min(TM_PROJ, seq)
    ws, tile_src = _even_weights(w_in, TN_PROJ)
    h = _inproj(x2, ws, tile_src, cos2, sin2, _even_tile_modes(TN_PROJ), tm, TN_PROJ)
    pe = jnp.stack([pe_k, pe_v])
    w1 = jnp.stack([w1k, w1v]).reshape(2, CMP_LEN, d, d).astype(BF16)
    w2 = jnp.stack([w2k, w2v]).astype(BF16)
    cmp_kv = _compress(h, pe, w1, w2, batch, seq)
    ocmp, sel = _cmp_topk(h, cmp_kv, batch, seq, TQ_NSA)
    o_a = _nsa_attn(h, sel, ocmp, cos2, sin2, batch, seq, TQ_NSA, min(TK_NSA, seq))
    o_b = _stick_breaking(h, batch, seq, min(T_SB, seq), SB_HEADS_PER_STEP, SB_TRI_BLOCK)
    return _outproj_ln([o_a, o_b], w_out.astype(BF16), x2, ln_g, ln_b, TM_OUT)


def _odd_layer(x2, batch, seq, w_in, lq1, lk1, lq2, lk2, gn_g, w_out, ln_g, ln_b, lambda_init, cos2, sin2):
    tm = min(TM_PROJ, seq)
    tile_src = [(0, t) for t in range(w_in.shape[1] // TN_PROJ)]
    h = _inproj(x2, [w_in.astype(BF16)], tile_src, cos2, sin2, _odd_tile_modes(TN_PROJ), tm, TN_PROJ)
    o = _diff_attn(h, gn_g, lq1, lk1, lq2, lk2, lambda_init, batch, seq, min(T_DIFF, seq), DIFF_HEADS_PER_STEP)
    return _outproj_ln([o], w_out.astype(BF16), x2, ln_g, ln_b, TM_OUT)


def kernel(x, ev_w_in, ev_pe_k, ev_pe_v, ev_w1_k, ev_w2_k, ev_w1_v, ev_w2_v, ev_w_out, ev_ln_g, ev_ln_b,
           od_w_in, od_lq1, od_lk1, od_lq2, od_lk2, od_gn_g, od_w_out, od_ln_g, od_ln_b):
    batch, seq, dm = x.shape
    cos2, sin2 = _rope_tables(seq)
    x2 = x.reshape(batch * seq, dm)
    for layer in range(DEPTH):
        i = layer // 2
        if layer % 2 == 0:
            x2 = _even_layer(x2, batch, seq, ev_w_in[i], ev_pe_k[i], ev_pe_v[i], ev_w1_k[i], ev_w2_k[i],
                             ev_w1_v[i], ev_w2_v[i], ev_w_out[i], ev_ln_g[i], ev_ln_b[i], cos2, sin2)
        else:
            lambda_init = 0.8 - 0.6 * math.exp(-0.3 * layer)
            x2 = _odd_layer(x2, batch, seq, od_w_in[i], od_lq1[i], od_lk1[i], od_lq2[i], od_lk2[i], od_gn_g[i],
                            od_w_out[i], od_ln_g[i], od_ln_b[i], lambda_init, cos2, sin2)
    return x2.reshape(batch, seq, dm)
```

```python
import functools
import math

import jax
import jax.numpy as jnp
from jax import lax
from jax.experimental import pallas as pl
from jax.experimental.pallas import tpu as pltpu

F32 = jnp.float32
BF16 = jnp.bfloat16

HEAD_DIM = 128
ROPE_THETA = 10000.0
NSA_HEADS = 8
NSA_KV_HEADS = 2
NSA_GROUP = NSA_HEADS // NSA_KV_HEADS
CMP_LEN = 32
CMP_STRIDE = 16
SEL_LEN = 64
SEL_SHIFT = SEL_LEN.bit_length() - 1
SEL_TOPK = 16
WIN = 512
SEL_FORCE = 1.0e4
SB_HEADS = 8
DIFF_HEADS = 8
DEPTH = 2
DEEPNORM_ALPHA = (2.0 * DEPTH) ** 0.25
LN_EPS = 1e-5
RMS_EPS = 1e-5
LOG2E = math.log2(math.e)
SCALE = HEAD_DIM ** -0.5 * LOG2E

LANES = 128
NEG = -1.0e30
OUT_ROW_CHUNK = 256
SOFTMAX_CHUNK = 64
VMEM_LIMIT = 56 * 1024 * 1024

MXU_TILE = 256
EV_QA, EV_KC, EV_VC, EV_KS, EV_VS, EV_KW, EV_VW, EV_GA = 0, 8, 10, 12, 14, 16, 18, 20
EV_GATE_A, EV_QB, EV_KB, EV_VB, EV_GATE_B = 24, 32, 40, 48, 56
EV_WIDTH = 64 * LANES


def _cparams(sem):
    return pltpu.CompilerParams(dimension_semantics=sem, vmem_limit_bytes=VMEM_LIMIT)


def _dot_nt(a, b):
    return lax.dot_general(a, b, (((1,), (1,)), ((), ())), preferred_element_type=F32)


def _dot(a, b):
    return jnp.dot(a, b, preferred_element_type=F32)


def _silu(x):
    return x * (1.0 / (1.0 + jnp.exp(-x)))


def _sigmoid(x):
    return 1.0 / (1.0 + jnp.exp(-x))


def _rope_f32(x, cos2, sin2):
    return x * cos2 + pltpu.roll(x, HEAD_DIM // 2, axis=1) * sin2


def _inproj_kernel(*refs, tile_modes, tile_src, w_is_nk):
    n_w = len(refs) - 5
    x_ref, w_refs, (cos_ref, sin_ref, o_ref, xb_ref) = refs[0], refs[1:1 + n_w], refs[1 + n_w:]
    n = pl.program_id(1)

    @pl.when(n == 0)
    def _():
        xb_ref[...] = x_ref[...].astype(BF16)

    variants = {}
    for tile, pat in enumerate(tile_modes):
        variants.setdefault((tile_src[tile][0], pat), []).append(tile)
    per_chunk = MXU_TILE // LANES
    for (w_idx, pat), tiles in variants.items():
        cond = functools.reduce(jnp.logical_or, [n == t for t in tiles])

        @pl.when(cond)
        def _(pat=pat, w_ref=w_refs[w_idx]):
            for cc in range(len(pat) // per_chunk):
                chunk = slice(cc * MXU_TILE, (cc + 1) * MXU_TILE)
                acc = _dot_nt(xb_ref[...], w_ref[chunk, :]) if w_is_nk else _dot(xb_ref[...], w_ref[:, chunk])
                for c2 in range(per_chunk):
                    c = cc * per_chunk + c2
                    rope, scale = pat[c]
                    y = acc[:, c2 * LANES:(c2 + 1) * LANES]
                    if rope:
                        y = _rope_f32(y, cos_ref[...], sin_ref[...])
                    if scale:
                        y = y * SCALE
                    o_ref[:, c * LANES:(c + 1) * LANES] = y.astype(o_ref.dtype)


def _inproj(x, ws, tile_src, w_is_nk, cos2, sin2, tile_modes, tm, tn):
    m, k = x.shape
    n_tiles = len(tile_src)
    s = cos2.shape[0]
    assert m % tm == 0 and s % tm == 0 and len(tile_modes) == n_tiles
    pos_tiles = s // tm

    def w_spec(a):
        table, last = [], [blk for (src, blk) in tile_src if src == a][0]
        for src, blk in tile_src:
            last = blk if src == a else last
            table.append(last)

        def block(j):
            blk = jnp.int32(table[0])
            for t in range(1, n_tiles):
                blk = jnp.where(j == t, jnp.int32(table[t]), blk)
            return blk

        if w_is_nk:
            return pl.BlockSpec((tn, k), lambda i, j: (block(j), 0))
        return pl.BlockSpec((k, tn), lambda i, j: (0, block(j)))

    return pl.pallas_call(
        functools.partial(_inproj_kernel, tile_modes=tile_modes, tile_src=tuple(tile_src), w_is_nk=w_is_nk),
        grid=(m // tm, n_tiles),
        in_specs=[pl.BlockSpec((tm, k), lambda i, j: (i, 0))] + [w_spec(a) for a in range(len(ws))] + [
            pl.BlockSpec((tm, LANES), lambda i, j: (i % pos_tiles, 0)),
            pl.BlockSpec((tm, LANES), lambda i, j: (i % pos_tiles, 0)),
        ],
        out_specs=pl.BlockSpec((tm, tn), lambda i, j: (i, j)),
        out_shape=jax.ShapeDtypeStruct((m, n_tiles * tn), BF16),
        scratch_shapes=[pltpu.VMEM((tm, k), BF16)],
        compiler_params=_cparams(("arbitrary", "arbitrary")),
        name="inproj",
    )(x, *ws, cos2, sin2)


def _outproj_ln_kernel(*refs, n_parts):
    o_refs = refs[:n_parts]
    w_ref, x_ref, g_ref, b_ref, out_ref = refs[n_parts:]
    kp = w_ref.shape[0] // n_parts
    for r in range(x_ref.shape[0] // OUT_ROW_CHUNK):
        rows = slice(r * OUT_ROW_CHUNK, (r + 1) * OUT_ROW_CHUNK)
        y = None
        for p in range(n_parts):
            part = _dot(o_refs[p][rows, :], w_ref[p * kp:(p + 1) * kp, :])
            y = part if y is None else y + part
        z = DEEPNORM_ALPHA * x_ref[rows, :] + y
        mu = jnp.mean(z, axis=-1, keepdims=True)
        zc = z - mu
        var = jnp.mean(zc * zc, axis=-1, keepdims=True)
        out_ref[rows, :] = zc * lax.rsqrt(var + LN_EPS) * g_ref[...] + b_ref[...]


def _outproj_ln(parts, w, x, g, b, tm):
    m, d = x.shape
    n_parts = len(parts)
    kp = w.shape[0] // n_parts
    return pl.pallas_call(
        functools.partial(_outproj_ln_kernel, n_parts=n_parts),
        grid=(m // tm,),
        in_specs=[pl.BlockSpec((tm, kp), lambda i: (i, 0)) for _ in parts] + [
            pl.BlockSpec(w.shape, lambda i: (0, 0)),
            pl.BlockSpec((tm, d), lambda i: (i, 0)),
            pl.BlockSpec((1, d), lambda i: (0, 0)),
            pl.BlockSpec((1, d), lambda i: (0, 0)),
        ],
        out_specs=pl.BlockSpec((tm, d), lambda i: (i, 0)),
        out_shape=jax.ShapeDtypeStruct((m, d), F32),
        compiler_params=_cparams(("arbitrary",)),
        name="outproj_ln",
    )(*parts, w, x, g.reshape(1, d), b.reshape(1, d))


def _compress_kernel(kv_ref, pe_ref, w1_ref, w2_ref, o_ref, pad_ref, *, seq):
    n_rows = seq // CMP_STRIDE
    pad_ref[0:seq, :] = kv_ref[...].astype(F32)
    pad_ref[seq:seq + CMP_STRIDE, :] = jnp.zeros((CMP_STRIDE, HEAD_DIM), F32)
    pre = jnp.zeros((n_rows, HEAD_DIM), F32)
    for l in range(CMP_LEN):
        rows = pad_ref[pl.ds(l, n_rows, stride=CMP_STRIDE), :] + pe_ref[l:l + 1, :]
        pre = pre + _dot(rows.astype(BF16), w1_ref[l])
    o_ref[...] = _dot(_silu(pre).astype(BF16), w2_ref[...]).astype(o_ref.dtype)


def _compress(h, pe, w1, w2, batch, seq):
    n_rows = seq // CMP_STRIDE
    hk = NSA_KV_HEADS
    return pl.pallas_call(
        functools.partial(_compress_kernel, seq=seq),
        grid=(2, batch, hk),
        in_specs=[
            pl.BlockSpec((seq, HEAD_DIM), lambda c, b, h: (b, EV_KC + c * hk + h)),
            pl.BlockSpec((None, CMP_LEN, HEAD_DIM), lambda c, b, h: (c, 0, 0)),
            pl.BlockSpec((None, CMP_LEN, HEAD_DIM, HEAD_DIM), lambda c, b, h: (c, 0, 0, 0)),
            pl.BlockSpec((None, HEAD_DIM, HEAD_DIM), lambda c, b, h: (c, 0, 0)),
        ],
        out_specs=pl.BlockSpec((None, None, n_rows, HEAD_DIM), lambda c, b, h: (c, b * hk + h, 0, 0)),
        out_shape=jax.ShapeDtypeStruct((2, batch * hk, n_rows, HEAD_DIM), BF16),
        scratch_shapes=[pltpu.VMEM((seq + CMP_STRIDE, HEAD_DIM), F32)],
        compiler_params=_cparams(("arbitrary", "arbitrary", "arbitrary")),
        name="nsa_compress",
    )(h, pe, w1, w2)


def _heads_to_rows(x, g):
    return jnp.concatenate([x[:, i * HEAD_DIM:(i + 1) * HEAD_DIM] for i in range(g)], axis=0)


def _rows_to_heads(x, g):
    r = x.shape[0] // g
    return jnp.concatenate([x[i * r:(i + 1) * r, :] for i in range(g)], axis=1)


def _cmp_topk_kernel(q_ref, kc_ref, vc_ref, ocmp_ref, sel_ref, *, tq, n_sel_blk, n_sel):
    g, hk = NSA_GROUP, NSA_KV_HEADS
    i = pl.program_id(1)
    q0 = i * tq
    n_rows = kc_ref.shape[1]
    t_col = q0 + lax.broadcasted_iota(jnp.int32, (tq, n_rows), 0)
    blk_end = lax.broadcasted_iota(jnp.int32, (tq, n_rows), 1) * CMP_STRIDE + (CMP_LEN - 1)
    vis = (blk_end <= t_col)[None]
    jj = lax.broadcasted_iota(jnp.int32, (n_sel_blk, n_rows), 0) * SEL_LEN
    cs = lax.broadcasted_iota(jnp.int32, (n_sel_blk, n_rows), 1) * CMP_STRIDE
    ov_t = jnp.where((cs < jj + SEL_LEN) & (cs + CMP_LEN > jj), 1.0, 0.0).astype(BF16)
    blk = lax.broadcasted_iota(jnp.int32, (n_sel_blk, tq), 0)
    j_cur = (q0 + lax.broadcasted_iota(jnp.int32, (n_sel_blk, tq), 1)) >> SEL_SHIFT
    forced = (blk == 0) | (blk == j_cur) | (blk == j_cur - 1)

    scores = []
    for h in range(hk):
        q4 = _heads_to_rows(q_ref[:, h * g * HEAD_DIM:(h + 1) * g * HEAD_DIM], g)
        sc = _dot_nt(q4, kc_ref[h]).reshape(g, tq, n_rows)
        sm = jnp.where(vis, sc, NEG)
        mx = jnp.max(sm, axis=-1, keepdims=True)
        e = jnp.where(vis, jnp.exp2(sm - mx), 0.0)
        den = jnp.sum(e, axis=-1, keepdims=True)
        p = e / jnp.where(den > 0.0, den, 1.0)
        o = _dot(p.reshape(g * tq, n_rows).astype(BF16), vc_ref[h])
        ocmp_ref[:, h * g * HEAD_DIM:(h + 1) * g * HEAD_DIM] = _rows_to_heads(o, g).astype(ocmp_ref.dtype)

        p_sum = jnp.sum(p, axis=0)
        p_hi = p_sum.astype(BF16)
        p_lo = (p_sum - p_hi.astype(F32)).astype(BF16)
        imp_t = _dot_nt(ov_t, p_hi) + _dot_nt(ov_t, p_lo)
        scores.append(jnp.where(forced, SEL_FORCE, jnp.where(blk <= j_cur, imp_t, -SEL_FORCE)))

    ranks = [jnp.zeros((n_sel_blk, tq), F32) for _ in range(hk)]
    for r in range(n_sel_blk):
        for h in range(hk):
            row = scores[h][r:r + 1, :]
            ahead = (row > scores[h]) | ((row == scores[h]) & (blk > r))
            ranks[h] = ranks[h] + jnp.where(ahead, 1.0, 0.0)
    for h in range(hk):
        sel_t = jnp.where(ranks[h] < float(n_sel), 1.0, 0.0)
        if n_sel_blk < LANES:
            sel_t = jnp.concatenate([sel_t, jnp.zeros((LANES - n_sel_blk, tq), F32)], axis=0)
        sel_ref[h] = sel_t.T.astype(sel_ref.dtype)


def _cmp_topk(h, cmp_kv, batch, seq, tq):
    hk = NSA_KV_HEADS
    nq = seq // tq
    n_rows = seq // CMP_STRIDE
    n_sel_blk = seq // SEL_LEN
    wq = NSA_HEADS * HEAD_DIM
    assert n_sel_blk <= LANES and tq == LANES
    n_sel = min(SEL_TOPK, n_sel_blk)
    ocmp, sel = pl.pallas_call(
        functools.partial(_cmp_topk_kernel, tq=tq, n_sel_blk=n_sel_blk, n_sel=n_sel),
        grid=(batch, nq),
        in_specs=[
            pl.BlockSpec((tq, wq), lambda b, i: (b * nq + i, EV_QA * LANES // wq)),
            pl.BlockSpec((None, hk, n_rows, HEAD_DIM), lambda b, i: (0, b, 0, 0)),
            pl.BlockSpec((None, hk, n_rows, HEAD_DIM), lambda b, i: (1, b, 0, 0)),
        ],
        out_specs=[
            pl.BlockSpec((tq, wq), lambda b, i: (b * nq + i, 0)),
            pl.BlockSpec((hk, tq, LANES), lambda b, i: (b, i, 0)),
        ],
        out_shape=[
            jax.ShapeDtypeStruct((batch * seq, wq), BF16),
            jax.ShapeDtypeStruct((batch * hk, seq, LANES), BF16),
        ],
        compiler_params=_cparams(("arbitrary", "arbitrary")),
        name="nsa_cmp_topk",
    )(h, cmp_kv, cmp_kv)
    return ocmp, sel.reshape(batch * hk * seq, LANES)


def _lane_tile(x, n):
    return x if n == 1 else jnp.concatenate([x] * n, axis=1)


def _softmax_init(m_ref, l_ref):
    m_ref[...] = jnp.full(m_ref.shape, NEG, F32)
    l_ref[...] = jnp.zeros(l_ref.shape, F32)


def _softmax_rows(s_ref, p_ref, al_ref, m_ref, l_ref, chunk):
    rows, tk = s_ref.shape
    for r in range(rows // chunk):
        sl = slice(r * chunk, (r + 1) * chunk)
        s = s_ref[sl, :]
        m_prev = m_ref[sl, :]
        m_new = jnp.maximum(m_prev, jnp.max(s, axis=1, keepdims=True))
        alpha = jnp.exp2(m_prev - m_new)
        p = jnp.exp2(s - _lane_tile(m_new, tk // LANES))
        l_ref[sl, :] = alpha * l_ref[sl, :] + jnp.sum(p, axis=1, keepdims=True)
        m_ref[sl, :] = m_new
        al_ref[sl, :] = alpha
        p_ref[sl, :] = p.astype(BF16)


def _two_slot_pipeline(n_steps, produce, consume):
    produce(0, 0, True)

    def body(k, _):
        for parity in range(2):
            @pl.when(k % 2 == parity)
            def _(parity=parity):
                produce(1 - parity, k + 1, False)
                consume(parity, k)
        return 0

    lax.fori_loop(0, n_steps - 1, body, 0)
    last = n_steps - 1
    for parity in range(2):
        @pl.when(last % 2 == parity)
        def _(parity=parity):
            consume(parity, last)


def _carried_pipeline(i, produce_first, produce, consume):
    before = (i * (i + 1) // 2) % 2

    @pl.when(i == 0)
    def _():
        produce_first(0, i)

    def body(k, _):
        for parity in range(2):
            @pl.when((before + k) % 2 == parity)
            def _(parity=parity):
                produce(1 - parity, k + 1)
                consume(parity, k)
        return 0

    lax.fori_loop(0, i, body, 0)
    for parity in range(2):
        @pl.when((before + i) % 2 == parity)
        def _(parity=parity):
            produce_first(1 - parity, i + 1)
            consume(parity, i)


def _nsa_attn_kernel(q_ref, sel0_ref, sel1_ref, ocmp_ref, ks_ref, vs_ref, kw_ref, vw_ref, cos_ref, sin_ref, ga_ref,
                     gate0_ref, gate1_ref, o_ref, m_ref, l_ref, al_ref, acc_ref, q4_ref, s_ref, p_ref, sw_ref,
                     pw_ref, ow_ref, *, tq, tk):
    g, hk = NSA_GROUP, NSA_KV_HEADS
    rows = g * tq
    i = pl.program_id(1)
    q0 = i * tq
    sel_refs = (sel0_ref, sel1_ref)
    gate_refs = (gate0_ref, gate1_ref)
    kv_cols = lambda h: slice(h * HEAD_DIM, (h + 1) * HEAD_DIM)
    cos2 = jnp.concatenate([cos_ref[...]] * g, axis=0)
    sin2 = jnp.concatenate([sin_ref[...]] * g, axis=0)
    for h in range(hk):
        q_h = q_ref[:, h * g * HEAD_DIM:(h + 1) * g * HEAD_DIM]
        q4_ref[h] = _rope_f32(_heads_to_rows(q_h, g).astype(F32), cos2, sin2).astype(BF16)

    span = WIN + tq
    w0 = pl.multiple_of(jnp.maximum(q0 - WIN, 0), tq)
    dist = (q0 + lax.broadcasted_iota(jnp.int32, (tq, span), 0)) - (w0 + lax.broadcasted_iota(jnp.int32, (tq, span), 1))
    in_window = (dist >= 0) & (dist < WIN)
    _softmax_init(m_ref, l_ref)
    for h in range(hk):
        s = _dot_nt(q4_ref[h], kw_ref[pl.ds(w0, span), kv_cols(h)])
        sw_ref[h] = jnp.where(in_window[None], s.reshape(g, tq, span), NEG).reshape(rows, span)
    for h in range(hk):
        _softmax_rows(sw_ref.at[h], pw_ref.at[h], al_ref.at[h], m_ref.at[h], l_ref.at[h], SOFTMAX_CHUNK)
    for h in range(hk):
        ow_ref[h] = _dot(pw_ref[h], vw_ref[pl.ds(w0, span), kv_cols(h)]) / l_ref[h]

    _softmax_init(m_ref, l_ref)
    acc_ref[...] = jnp.zeros(acc_ref.shape, F32)

    def produce(slot, j, first):
        del first
        k0 = pl.multiple_of(j * tk, tk)
        blk_row = lax.broadcasted_iota(jnp.int32, (LANES, tk), 0)
        blk_of_lane = lax.broadcasted_iota(jnp.int32, (LANES, tk), 1) >> SEL_SHIFT
        expand = jnp.where(blk_row == j * (tk // SEL_LEN) + blk_of_lane, 1.0, 0.0).astype(BF16)
        causal = k0 + lax.broadcasted_iota(jnp.int32, (tq, tk), 1) <= q0 + lax.broadcasted_iota(jnp.int32, (tq, tk), 0)
        for h in range(hk):
            s = _dot_nt(q4_ref[h], ks_ref[pl.ds(k0, tk), kv_cols(h)])
            picked = _dot(sel_refs[h][...], expand)
            valid = (picked > 0.5) & causal
            s_ref[slot, h] = jnp.where(valid[None], s.reshape(g, tq, tk), NEG).reshape(rows, tk)

    def consume(slot, j):
        k0 = pl.multiple_of(j * tk, tk)
        for h in range(hk):
            _softmax_rows(s_ref.at[slot, h], p_ref.at[h], al_ref.at[h], m_ref.at[h], l_ref.at[h], SOFTMAX_CHUNK)
        for h in range(hk):
            acc_ref[h] = acc_ref[h] * al_ref[h] + _dot(p_ref[h], vs_ref[pl.ds(k0, tk), kv_cols(h)])

    _two_slot_pipeline((q0 + tq - 1) // tk + 1, produce, consume)

    gt = _sigmoid(ga_ref[...].astype(F32))
    for h in range(hk):
        for a in range(g):
            cols = slice((h * g + a) * HEAD_DIM, (h * g + a + 1) * HEAD_DIM)
            head = slice(a * tq, (a + 1) * tq)
            lane = (h * g + a) * 3
            o_sel = acc_ref[h, head, :] / l_ref[h, head, :]
            mix = (gt[:, lane:lane + 1] * ocmp_ref[:, cols].astype(F32) + gt[:, lane + 1:lane + 2] * o_sel
                   + gt[:, lane + 2:lane + 3] * ow_ref[h, head, :])
            gate = _silu(gate_refs[h][:, a * HEAD_DIM:(a + 1) * HEAD_DIM].astype(F32))
            o_ref[:, cols] = (mix * gate).astype(o_ref.dtype)


def _nsa_attn(h, sel, ocmp, cos2, sin2, batch, seq, tq, tk):
    hk, g = NSA_KV_HEADS, NSA_GROUP
    nq = seq // tq
    rows = g * tq
    wq = NSA_HEADS * HEAD_DIM
    wkv = hk * HEAD_DIM
    assert WIN % tq == 0 and tk % SEL_LEN == 0 and seq % tk == 0 and seq >= WIN + tq and tk % tq == 0 and hk == 2
    slab = lambda col: pl.BlockSpec((seq, wkv), lambda b, i: (b, col // hk))
    sel_spec = lambda hh: pl.BlockSpec((tq, LANES), lambda b, i: ((b * hk + hh) * nq + i, 0))
    return pl.pallas_call(
        functools.partial(_nsa_attn_kernel, tq=tq, tk=tk),
        grid=(batch, nq),
        in_specs=[
            pl.BlockSpec((tq, wq), lambda b, i: (b * nq + i, EV_QA * LANES // wq)),
            sel_spec(0), sel_spec(1),
            pl.BlockSpec((tq, wq), lambda b, i: (b * nq + i, 0)),
            slab(EV_KS), slab(EV_VS), slab(EV_KW), slab(EV_VW),
            pl.BlockSpec((tq, LANES), lambda b, i: (i, 0)),
            pl.BlockSpec((tq, LANES), lambda b, i: (i, 0)),
            pl.BlockSpec((tq, LANES), lambda b, i: (b * nq + i, EV_GA)),
            pl.BlockSpec((tq, g * HEAD_DIM), lambda b, i: (b * nq + i, EV_GATE_A // g)),
            pl.BlockSpec((tq, g * HEAD_DIM), lambda b, i: (b * nq + i, EV_GATE_A // g + 1)),
        ],
        out_specs=pl.BlockSpec((tq, wq), lambda b, i: (b * nq + i, 0)),
        out_shape=jax.ShapeDtypeStruct((batch * seq, wq), BF16),
        scratch_shapes=[
            pltpu.VMEM((hk, rows, LANES), F32), pltpu.VMEM((hk, rows, LANES), F32), pltpu.VMEM((hk, rows, LANES), F32),
            pltpu.VMEM((hk, rows, HEAD_DIM), F32), pltpu.VMEM((hk, rows, HEAD_DIM), BF16),
            pltpu.VMEM((2, hk, rows, tk), F32), pltpu.VMEM((hk, rows, tk), BF16),
            pltpu.VMEM((hk, rows, WIN + tq), F32), pltpu.VMEM((hk, rows, WIN + tq), BF16),
            pltpu.VMEM((hk, rows, HEAD_DIM), F32),
        ],
        compiler_params=_cparams(("arbitrary", "arbitrary")),
        name="nsa_attn",
    )(h, sel, sel, ocmp, h, h, h, h, cos2, sin2, h, h, h)


def _sb_kernel(q_ref, k_ref, v_ref, gate_ref, tri_ref, o_ref, later_ref, acc_ref, z_ref, zs_ref, sp16_ref, sp0_ref,
               after_ref, a_ref, *, t):
    i = pl.program_id(2)
    n_q = pl.num_programs(2)
    chunk = SOFTMAX_CHUNK
    nh = acc_ref.shape[0]
    tb = tri_ref.shape[0]
    nb = t // tb
    later_ref[...] = jnp.zeros(later_ref.shape, F32)
    acc_ref[...] = jnp.zeros(acc_ref.shape, F32)
    head_cols = lambda h: slice(h * HEAD_DIM, (h + 1) * HEAD_DIM)

    def logits(slot, qi, kj, diagonal):
        q0 = pl.multiple_of(qi * t, t)
        k0 = pl.multiple_of(kj * t, t)
        for h in range(nh):
            z = _dot_nt(q_ref[pl.ds(q0, t), head_cols(h)], k_ref[pl.ds(k0, t), head_cols(h)])
            if diagonal:
                strict = lax.broadcasted_iota(jnp.int32, (t, t), 1) < lax.broadcasted_iota(jnp.int32, (t, t), 0)
                z = jnp.where(strict, z, NEG)
            z_ref[slot, h] = z

    def produce_first(slot, qi):
        qi = jnp.minimum(qi, n_q - 1)
        logits(slot, qi, qi, True)

    def produce(slot, k):
        logits(slot, i, i - k, False)

    def consume(slot, k):
        k0 = pl.multiple_of((i - k) * t, t)
        for h in range(nh):
            for r in range(t // chunk):
                sl = slice(r * chunk, (r + 1) * chunk)
                z = z_ref[slot, h, sl, :]
                sp = jnp.maximum(z, 0.0) + jnp.log(1.0 + jnp.exp2(jnp.minimum(z, -z))) * LOG2E
                zs_ref[h, sl, :] = z - sp
                sp16_ref[h, sl, :] = sp.astype(BF16)
                sp0_ref[h, sl, :] = jnp.concatenate([sp[:, b * tb:b * tb + LANES] for b in range(nb)], axis=1)
        for h in range(nh):
            for b in range(nb):
                bc = slice(b * tb, (b + 1) * tb)
                after_ref[h, :, bc] = _dot(sp16_ref[h, :, bc], tri_ref[...])
        for h in range(nh):
            for r in range(t // chunk):
                sl = slice(r * chunk, (r + 1) * chunk)
                run = later_ref[h, sl, :]
                for b in reversed(range(nb)):
                    bc = slice(b * tb, (b + 1) * tb)
                    after = after_ref[h, sl, bc]
                    a_ref[h, sl, bc] = jnp.exp2(zs_ref[h, sl, bc] - after - _lane_tile(run, tb // LANES)).astype(BF16)
                    run = run + (after[:, 0:1] + sp0_ref[h, sl, b * LANES:b * LANES + 1])
                later_ref[h, sl, :] = run
            acc_ref[h] += _dot(a_ref[h], v_ref[pl.ds(k0, t), head_cols(h)])

    _carried_pipeline(i, produce_first, produce, consume)
    for h in range(nh):
        o_ref[:, head_cols(h)] = (acc_ref[h] * _silu(gate_ref[:, head_cols(h)].astype(F32))).astype(o_ref.dtype)


def _stick_breaking(h, batch, seq, t, nh, tb):
    nq = seq // t
    w = nh * HEAD_DIM
    assert SB_HEADS % nh == 0 and t % tb == 0
    tri = (lax.broadcasted_iota(jnp.int32, (tb, tb), 0) > lax.broadcasted_iota(jnp.int32, (tb, tb), 1)).astype(BF16)
    col = lambda first_block: first_block // nh
    return pl.pallas_call(
        functools.partial(_sb_kernel, t=t),
        grid=(batch, SB_HEADS // nh, nq),
        in_specs=[
            pl.BlockSpec((seq, w), lambda b, h_, i: (b, col(EV_QB) + h_)),
            pl.BlockSpec((seq, w), lambda b, h_, i: (b, col(EV_KB) + h_)),
            pl.BlockSpec((seq, w), lambda b, h_, i: (b, col(EV_VB) + h_)),
            pl.BlockSpec((t, w), lambda b, h_, i: (b * nq + i, col(EV_GATE_B) + h_)),
            pl.BlockSpec((tb, tb), lambda b, h_, i: (0, 0)),
        ],
        out_specs=pl.BlockSpec((t, w), lambda b, h_, i: (b * nq + i, h_)),
        out_shape=jax.ShapeDtypeStruct((batch * seq, SB_HEADS * HEAD_DIM), BF16),
        scratch_shapes=[
            pltpu.VMEM((nh, t, LANES), F32), pltpu.VMEM((nh, t, HEAD_DIM), F32), pltpu.VMEM((2, nh, t, t), F32),
            pltpu.VMEM((nh, t, t), F32), pltpu.VMEM((nh, t, t), BF16), pltpu.VMEM((nh, t, (t // tb) * LANES), F32),
            pltpu.VMEM((nh, t, t), F32), pltpu.VMEM((nh, t, t), BF16),
        ],
        compiler_params=_cparams(("arbitrary", "arbitrary", "arbitrary")),
        name="stick_breaking",
    )(h, h, h, h, tri)


def _diff_kernel(q_ref, k_ref, v_ref, gate_ref, gn_ref, lq1_ref, lk1_ref, lq2_ref, lk2_ref, o_ref,
                 m_ref, l_ref, acc_ref, s_ref, p_ref, al_ref, *, t, lambda_init):
    d = HEAD_DIM
    i = pl.program_id(2)
    n_q = pl.num_programs(2)
    n_str = m_ref.shape[0]
    _softmax_init(m_ref, l_ref)
    acc_ref[...] = jnp.zeros(acc_ref.shape, F32)

    def scores(slot, qi, kj, diagonal):
        q0 = pl.multiple_of(qi * t, t)
        k0 = pl.multiple_of(kj * t, t)
        for u in range(n_str):
            s = _dot_nt(q_ref[pl.ds(q0, t), u * d:(u + 1) * d], k_ref[pl.ds(k0, t), u * d:(u + 1) * d])
            if diagonal:
                causal = lax.broadcasted_iota(jnp.int32, (t, t), 1) <= lax.broadcasted_iota(jnp.int32, (t, t), 0)
                s = jnp.where(causal, s, NEG)
            s_ref[slot, u] = s

    def produce_first(slot, qi):
        qi = jnp.minimum(qi, n_q - 1)
        scores(slot, qi, qi, True)

    def produce(slot, k):
        scores(slot, i, k - 1, False)

    def consume(slot, k):
        k0 = pl.multiple_of(jnp.where(k == 0, i, k - 1) * t, t)
        for u in range(n_str):
            _softmax_rows(s_ref.at[slot, u], p_ref.at[u], al_ref.at[u], m_ref.at[u], l_ref.at[u], SOFTMAX_CHUNK)
        for u in range(n_str):
            vv = v_ref[pl.ds(k0, t), (u // 2) * 2 * d:(u // 2 + 1) * 2 * d]
            acc_ref[u] = acc_ref[u] * _lane_tile(al_ref[u], 2) + _dot(p_ref[u], vv)

    _carried_pipeline(i, produce_first, produce, consume)
    lam = (jnp.exp(jnp.sum(lq1_ref[...] * lk1_ref[...], axis=-1, keepdims=True))
           - jnp.exp(jnp.sum(lq2_ref[...] * lk2_ref[...], axis=-1, keepdims=True)) + lambda_init)
    for hh in range(n_str // 2):
        cols = slice(hh * 2 * d, (hh + 1) * 2 * d)
        inv = [1.0 / _lane_tile(l_ref[2 * hh + c], 2) for c in range(2)]
        o = acc_ref[2 * hh] * inv[0] - lam * (acc_ref[2 * hh + 1] * inv[1])
        o = o * lax.rsqrt(jnp.mean(o * o, axis=-1, keepdims=True) + RMS_EPS)
        o = o * gn_ref[hh:hh + 1, :] * (1.0 - lambda_init)
        o_ref[:, cols] = (o * _silu(gate_ref[:, cols].astype(F32))).astype(o_ref.dtype)


def _diff_attn(h, gn_g, lq1, lk1, lq2, lk2, lambda_init, batch, seq, t, hps):
    nh = DIFF_HEADS
    nq = seq // t
    w = hps * 2 * HEAD_DIM
    ng = nh // hps
    assert nh % hps == 0
    vec = pl.BlockSpec((1, HEAD_DIM), lambda b, h_, i: (0, 0))
    r = lambda a: a.reshape(1, HEAD_DIM)
    return pl.pallas_call(
        functools.partial(_diff_kernel, t=t, lambda_init=lambda_init),
        grid=(batch, ng, nq),
        in_specs=[
            pl.BlockSpec((seq, w), lambda b, h_, i: (b, h_)),
            pl.BlockSpec((seq, w), lambda b, h_, i: (b, ng + h_)),
            pl.BlockSpec((seq, w), lambda b, h_, i: (b, 2 * ng + h_)),
            pl.BlockSpec((t, w), lambda b, h_, i: (b * nq + i, 3 * ng + h_)),
            pl.BlockSpec((None, hps, 2 * HEAD_DIM), lambda b, h_, i: (h_, 0, 0)),
            vec, vec, vec, vec,
        ],
        out_specs=pl.BlockSpec((t, w), lambda b, h_, i: (b * nq + i, h_)),
        out_shape=jax.ShapeDtypeStruct((batch * seq, nh * 2 * HEAD_DIM), BF16),
        scratch_shapes=[
            pltpu.VMEM((2 * hps, t, LANES), F32), pltpu.VMEM((2 * hps, t, LANES), F32),
            pltpu.VMEM((2 * hps, t, 2 * HEAD_DIM), F32), pltpu.VMEM((2, 2 * hps, t, t), F32),
            pltpu.VMEM((2 * hps, t, t), BF16), pltpu.VMEM((2 * hps, t, LANES), F32),
        ],
        compiler_params=_cparams(("arbitrary", "arbitrary", "arbitrary")),
        name="diff_attn",
    )(h, h, h, h, gn_g.reshape(ng, hps, 2 * HEAD_DIM), r(lq1), r(lk1), r(lq2), r(lk2))


def _rope_tables(seq):
    pos = jnp.arange(seq, dtype=F32)
    inv = ROPE_THETA ** (-jnp.arange(0, HEAD_DIM, 2, dtype=F32) / HEAD_DIM)
    ang = pos[:, None] * inv[None, :]
    cos, sin = jnp.cos(ang), jnp.sin(ang)
    return jnp.concatenate([cos, cos], axis=-1), jnp.concatenate([-sin, sin], axis=-1)


def _even_weights_nk(w_in, tn):
    tail0 = EV_GA * LANES + 3 * NSA_HEADS
    w_t = w_in.T.astype(BF16)
    n_head, n_tail = EV_GATE_A * LANES // tn, (w_t.shape[0] - tail0) // tn
    assert EV_GATE_A * LANES % tn == 0 and (n_head + n_tail) * tn == EV_WIDTH
    return [w_t, w_t[tail0:]], [(0, t) for t in range(n_head)] + [(1, t) for t in range(n_tail)]


def _even_tile_modes(tn):
    modes = [(False, False)] * (EV_WIDTH // LANES)
    for c in range(EV_QA, EV_QA + NSA_HEADS):
        modes[c] = (False, True)
    for c in list(range(EV_KS, EV_KS + NSA_KV_HEADS)) + list(range(EV_KW, EV_KW + NSA_KV_HEADS)):
        modes[c] = (True, False)
    for c in range(EV_QB, EV_QB + SB_HEADS):
        modes[c] = (False, True)
    per = tn // LANES
    return tuple(tuple(modes[i * per:(i + 1) * per]) for i in range(len(modes) // per))


def _odd_tile_modes(tn):
    nh = 2 * DIFF_HEADS
    modes = [(True, True)] * nh + [(True, False)] * nh + [(False, False)] * (2 * nh)
    per = tn // LANES
    return tuple(tuple(modes[i * per:(i + 1) * per]) for i in range(len(modes) // per))


TM_PROJ, TN_PROJ = 1024, 1024
TM_OUT = 512
TQ_NSA = 128
TQ_NSA_ATTN = 128
TK_NSA = 512
T_SB = 512
SB_HEADS_PER_STEP = 2
SB_TRI_BLOCK = 256
T_DIFF = 512
DIFF_HEADS_PER_STEP = 1


def _even_layer(x2, batch, seq, w_in, pe_k, pe_v, w1k, w2k, w1v, w2v, w_out, ln_g, ln_b, cos2, sin2):
    d = HEAD_DIM
    tm = min(TM_PROJ, seq)
    ws, tile_src = _even_weights_nk(w_in, TN_PROJ)
    h = _inproj(x2, ws, tile_src, True, cos2, sin2, _even_tile_modes(TN_PROJ), tm, TN_PROJ)
    pe = jnp.stack([pe_k, pe_v])
    w1 = jnp.stack([w1k, w1v]).reshape(2, CMP_LEN, d, d).astype(BF16)
    w2 = jnp.stack([w2k, w2v]).astype(BF16)
    cmp_kv = _compress(h, pe, w1, w2, batch, seq)
    ocmp, sel = _cmp_topk(h, cmp_kv, batch, seq, TQ_NSA)
    o_a = _nsa_attn(h, sel, ocmp, cos2, sin2, batch, seq, TQ_NSA_ATTN, min(TK_NSA, seq))
    o_b = _stick_breaking(h, batch, seq, min(T_SB, seq), SB_HEADS_PER_STEP, SB_TRI_BLOCK)
    return _outproj_ln([o_a, o_b], w_out.astype(BF16), x2, ln_g, ln_b, TM_OUT)


def _odd_layer(x2, batch, seq, w_in, lq1, lk1, lq2, lk2, gn_g, w_out, ln_g, ln_b, lambda_init, cos2, sin2):
    tm = min(TM_PROJ, seq)
    tile_src = [(0, t) for t in range(w_in.shape[1] // TN_PROJ)]
    h = _inproj(x2, [w_in.astype(BF16)], tile_src, False, cos2, sin2, _odd_tile_modes(TN_PROJ), tm, TN_PROJ)
    o = _diff_attn(h, gn_g, lq1, lk1, lq2, lk2, lambda_init, batch, seq, min(T_DIFF, seq), DIFF_HEADS_PER_STEP)
    return _outproj_ln([o], w_out.astype(BF16), x2, ln_g, ln_b, TM_OUT)


def kernel(x, ev_w_in, ev_pe_k, ev_pe_v, ev_w1_k, ev_w2_k, ev_w1_v, ev_w2_v, ev_w_out, ev_ln_g, ev_ln_b,
           od_w_in, od_lq1, od_lk1, od_lq2, od_lk2, od_gn_g, od_w_out, od_ln_g, od_ln_b):
    batch, seq, dm = x.shape
    cos2, sin2 = _rope_tables(seq)
    x2 = x.reshape(batch * seq, dm)
    for layer in range(DEPTH):
        i = layer // 2
        if layer % 2 == 0:
            x2 = _even_layer(x2, batch, seq, ev_w_in[i], ev_pe_k[i], ev_pe_v[i], ev_w1_k[i], ev_w2_k[i],
                             ev_w1_v[i], ev_w2_v[i], ev_w_out[i], ev_ln_g[i], ev_ln_b[i], cos2, sin2)
        else:
            lambda_init = 0.8 - 0.6 * math.exp(-0.3 * layer)
            x2 = _odd_layer(x2, batch, seq, od_w_in[i], od_lq1[i], od_lk1[i], od_lq2[i], od_lk2[i], od_gn_g[i],
                            od_w_out[i], od_ln_g[i], od_ln_b[i], lambda_init, cos2, sin2)
    return x2.reshape(batch, seq, dm)
```

```python
import functools
import math

import jax
import jax.numpy as jnp
from jax import lax
from jax.experimental import pallas as pl
from jax.experimental.pallas import tpu as pltpu

F32 = jnp.float32
BF16 = jnp.bfloat16

HEAD_DIM = 128
ROPE_THETA = 10000.0
NSA_HEADS = 8
NSA_KV_HEADS = 2
NSA_GROUP = NSA_HEADS // NSA_KV_HEADS
CMP_LEN = 32
CMP_STRIDE = 16
SEL_LEN = 64
SEL_SHIFT = SEL_LEN.bit_length() - 1
SEL_TOPK = 16
WIN = 512
SEL_FORCE = 1.0e4
SB_HEADS = 8
DIFF_HEADS = 8
DEPTH = 2
DEEPNORM_ALPHA = (2.0 * DEPTH) ** 0.25
LN_EPS = 1e-5
RMS_EPS = 1e-5
LOG2E = math.log2(math.e)
SCALE = HEAD_DIM ** -0.5 * LOG2E

LANES = 128
NEG = -1.0e30
OUT_ROW_CHUNK = 256
SOFTMAX_CHUNK = 64
VMEM_LIMIT = 56 * 1024 * 1024

MXU_TILE = 256
EV_QA, EV_KC, EV_VC, EV_KS, EV_VS, EV_KW, EV_VW, EV_GA = 0, 8, 10, 12, 14, 16, 18, 20
EV_GATE_A, EV_QB, EV_KB, EV_VB, EV_GATE_B = 24, 32, 40, 48, 56
EV_WIDTH = 64 * LANES


def _cparams(sem):
    return pltpu.CompilerParams(dimension_semantics=sem, vmem_limit_bytes=VMEM_LIMIT)


def _dot_nt(a, b):
    return lax.dot_general(a, b, (((1,), (1,)), ((), ())), preferred_element_type=F32)


def _dot(a, b):
    return jnp.dot(a, b, preferred_element_type=F32)


def _silu(x):
    return x * (1.0 / (1.0 + jnp.exp(-x)))


def _sigmoid(x):
    return 1.0 / (1.0 + jnp.exp(-x))


def _rope_f32(x, cos2, sin2):
    return x * cos2 + pltpu.roll(x, HEAD_DIM // 2, axis=1) * sin2


def _inproj_kernel(*refs, tile_modes, tile_src, w_is_nk):
    n_w = len(refs) - 5
    x_ref, w_refs, (cos_ref, sin_ref, o_ref, xb_ref) = refs[0], refs[1:1 + n_w], refs[1 + n_w:]
    n = pl.program_id(1)

    @pl.when(n == 0)
    def _():
        xb_ref[...] = x_ref[...].astype(BF16)

    variants = {}
    for tile, pat in enumerate(tile_modes):
        variants.setdefault((tile_src[tile][0], pat), []).append(tile)
    per_chunk = MXU_TILE // LANES
    for (w_idx, pat), tiles in variants.items():
        cond = functools.reduce(jnp.logical_or, [n == t for t in tiles])

        @pl.when(cond)
        def _(pat=pat, w_ref=w_refs[w_idx]):
            for cc in range(len(pat) // per_chunk):
                chunk = slice(cc * MXU_TILE, (cc + 1) * MXU_TILE)
                wc = (w_ref[chunk, :] if w_is_nk else w_ref[:, chunk]).astype(BF16)
                acc = _dot_nt(xb_ref[...], wc) if w_is_nk else _dot(xb_ref[...], wc)
                for c2 in range(per_chunk):
                    c = cc * per_chunk + c2
                    rope, scale = pat[c]
                    y = acc[:, c2 * LANES:(c2 + 1) * LANES]
                    if rope:
                        y = _rope_f32(y, cos_ref[...], sin_ref[...])
                    if scale:
                        y = y * SCALE
                    o_ref[:, c * LANES:(c + 1) * LANES] = y.astype(o_ref.dtype)


def _inproj(x, ws, tile_src, w_is_nk, cos2, sin2, tile_modes, tm, tn):
    m, k = x.shape
    n_tiles = len(tile_src)
    s = cos2.shape[0]
    assert m % tm == 0 and s % tm == 0 and len(tile_modes) == n_tiles
    pos_tiles = s // tm

    def w_spec(a):
        table, last = [], [blk for (src, blk) in tile_src if src == a][0]
        for src, blk in tile_src:
            last = blk if src == a else last
            table.append(last)

        def block(j):
            blk = jnp.int32(table[0])
            for t in range(1, n_tiles):
                blk = jnp.where(j == t, jnp.int32(table[t]), blk)
            return blk

        if w_is_nk:
            return pl.BlockSpec((tn, k), lambda i, j: (block(j), 0))
        return pl.BlockSpec((k, tn), lambda i, j: (0, block(j)))

    return pl.pallas_call(
        functools.partial(_inproj_kernel, tile_modes=tile_modes, tile_src=tuple(tile_src), w_is_nk=w_is_nk),
        grid=(m // tm, n_tiles),
        in_specs=[pl.BlockSpec((tm, k), lambda i, j: (i, 0))] + [w_spec(a) for a in range(len(ws))] + [
            pl.BlockSpec((tm, LANES), lambda i, j: (i % pos_tiles, 0)),
            pl.BlockSpec((tm, LANES), lambda i, j: (i % pos_tiles, 0)),
        ],
        out_specs=pl.BlockSpec((tm, tn), lambda i, j: (i, j)),
        out_shape=jax.ShapeDtypeStruct((m, n_tiles * tn), BF16),
        scratch_shapes=[pltpu.VMEM((tm, k), BF16)],
        compiler_params=_cparams(("arbitrary", "arbitrary")),
        name="inproj",
    )(x, *ws, cos2, sin2)


def _outproj_ln_kernel(*refs, n_parts):
    o_refs = refs[:n_parts]
    w_ref, x_ref, g_ref, b_ref, out_ref = refs[n_parts:]
    kp = w_ref.shape[0] // n_parts
    for r in range(x_ref.shape[0] // OUT_ROW_CHUNK):
        rows = slice(r * OUT_ROW_CHUNK, (r + 1) * OUT_ROW_CHUNK)
        y = None
        for p in range(n_parts):
            part = _dot(o_refs[p][rows, :], w_ref[p * kp:(p + 1) * kp, :])
            y = part if y is None else y + part
        z = DEEPNORM_ALPHA * x_ref[rows, :] + y
        mu = jnp.mean(z, axis=-1, keepdims=True)
        zc = z - mu
        var = jnp.mean(zc * zc, axis=-1, keepdims=True)
        out_ref[rows, :] = zc * lax.rsqrt(var + LN_EPS) * g_ref[...] + b_ref[...]


def _outproj_ln(parts, w, x, g, b, tm):
    m, d = x.shape
    n_parts = len(parts)
    kp = w.shape[0] // n_parts
    return pl.pallas_call(
        functools.partial(_outproj_ln_kernel, n_parts=n_parts),
        grid=(m // tm,),
        in_specs=[pl.BlockSpec((tm, kp), lambda i: (i, 0)) for _ in parts] + [
            pl.BlockSpec(w.shape, lambda i: (0, 0)),
            pl.BlockSpec((tm, d), lambda i: (i, 0)),
            pl.BlockSpec((1, d), lambda i: (0, 0)),
            pl.BlockSpec((1, d), lambda i: (0, 0)),
        ],
        out_specs=pl.BlockSpec((tm, d), lambda i: (i, 0)),
        out_shape=jax.ShapeDtypeStruct((m, d), F32),
        compiler_params=_cparams(("arbitrary",)),
        name="outproj_ln",
    )(*parts, w, x, g.reshape(1, d), b.reshape(1, d))


def _compress_kernel(kv_ref, pe_ref, w1_ref, w2_ref, o_ref, pad_ref, *, seq):
    n_rows = seq // CMP_STRIDE
    pad_ref[0:seq, :] = kv_ref[...].astype(F32)
    pad_ref[seq:seq + CMP_STRIDE, :] = jnp.zeros((CMP_STRIDE, HEAD_DIM), F32)
    pre = jnp.zeros((n_rows, HEAD_DIM), F32)
    for l in range(CMP_LEN):
        rows = pad_ref[pl.ds(l, n_rows, stride=CMP_STRIDE), :] + pe_ref[l:l + 1, :]
        pre = pre + _dot(rows.astype(BF16), w1_ref[l])
    o_ref[...] = _dot(_silu(pre).astype(BF16), w2_ref[...]).astype(o_ref.dtype)


def _compress(h, pe, w1, w2, batch, seq):
    n_rows = seq // CMP_STRIDE
    hk = NSA_KV_HEADS
    return pl.pallas_call(
        functools.partial(_compress_kernel, seq=seq),
        grid=(2, batch, hk),
        in_specs=[
            pl.BlockSpec((seq, HEAD_DIM), lambda c, b, h: (b, EV_KC + c * hk + h)),
            pl.BlockSpec((None, CMP_LEN, HEAD_DIM), lambda c, b, h: (c, 0, 0)),
            pl.BlockSpec((None, CMP_LEN, HEAD_DIM, HEAD_DIM), lambda c, b, h: (c, 0, 0, 0)),
            pl.BlockSpec((None, HEAD_DIM, HEAD_DIM), lambda c, b, h: (c, 0, 0)),
        ],
        out_specs=pl.BlockSpec((None, None, n_rows, HEAD_DIM), lambda c, b, h: (c, b * hk + h, 0, 0)),
        out_shape=jax.ShapeDtypeStruct((2, batch * hk, n_rows, HEAD_DIM), BF16),
        scratch_shapes=[pltpu.VMEM((seq + CMP_STRIDE, HEAD_DIM), F32)],
        compiler_params=_cparams(("arbitrary", "arbitrary", "arbitrary")),
        name="nsa_compress",
    )(h, pe, w1, w2)


def _heads_to_rows(x, g):
    return jnp.concatenate([x[:, i * HEAD_DIM:(i + 1) * HEAD_DIM] for i in range(g)], axis=0)


def _rows_to_heads(x, g):
    r = x.shape[0] // g
    return jnp.concatenate([x[i * r:(i + 1) * r, :] for i in range(g)], axis=1)


def _cmp_topk_kernel(q_ref, kc_ref, vc_ref, ocmp_ref, sel_ref, *, tq, n_sel_blk, n_sel):
    g, hk = NSA_GROUP, NSA_KV_HEADS
    i = pl.program_id(1)
    q0 = i * tq
    n_rows = kc_ref.shape[1]
    t_col = q0 + lax.broadcasted_iota(jnp.int32, (tq, n_rows), 0)
    blk_end = lax.broadcasted_iota(jnp.int32, (tq, n_rows), 1) * CMP_STRIDE + (CMP_LEN - 1)
    vis = (blk_end <= t_col)[None]
    jj = lax.broadcasted_iota(jnp.int32, (n_sel_blk, n_rows), 0) * SEL_LEN
    cs = lax.broadcasted_iota(jnp.int32, (n_sel_blk, n_rows), 1) * CMP_STRIDE
    ov_t = jnp.where((cs < jj + SEL_LEN) & (cs + CMP_LEN > jj), 1.0, 0.0).astype(BF16)
    blk = lax.broadcasted_iota(jnp.int32, (n_sel_blk, tq), 0)
    j_cur = (q0 + lax.broadcasted_iota(jnp.int32, (n_sel_blk, tq), 1)) >> SEL_SHIFT
    forced = (blk == 0) | (blk == j_cur) | (blk == j_cur - 1)

    scores = []
    for h in range(hk):
        q4 = _heads_to_rows(q_ref[:, h * g * HEAD_DIM:(h + 1) * g * HEAD_DIM], g)
        sc = _dot_nt(q4, kc_ref[h]).reshape(g, tq, n_rows)
        sm = jnp.where(vis, sc, NEG)
        mx = jnp.max(sm, axis=-1, keepdims=True)
        e = jnp.where(vis, jnp.exp2(sm - mx), 0.0)
        den = jnp.sum(e, axis=-1, keepdims=True)
        p = e / jnp.where(den > 0.0, den, 1.0)
        o = _dot(p.reshape(g * tq, n_rows).astype(BF16), vc_ref[h])
        ocmp_ref[:, h * g * HEAD_DIM:(h + 1) * g * HEAD_DIM] = _rows_to_heads(o, g).astype(ocmp_ref.dtype)

        p_sum = jnp.sum(p, axis=0)
        p_hi = p_sum.astype(BF16)
        p_lo = (p_sum - p_hi.astype(F32)).astype(BF16)
        imp_t = _dot_nt(ov_t, p_hi) + _dot_nt(ov_t, p_lo)
        scores.append(jnp.where(forced, SEL_FORCE, jnp.where(blk <= j_cur, imp_t, -SEL_FORCE)))

    ranks = [jnp.zeros((n_sel_blk, tq), F32) for _ in range(hk)]
    for r in range(n_sel_blk):
        for h in range(hk):
            row = scores[h][r:r + 1, :]
            ahead = (row > scores[h]) | ((row == scores[h]) & (blk > r))
            ranks[h] = ranks[h] + jnp.where(ahead, 1.0, 0.0)
    for h in range(hk):
        sel_t = jnp.where(ranks[h] < float(n_sel), 1.0, 0.0)
        if n_sel_blk < LANES:
            sel_t = jnp.concatenate([sel_t, jnp.zeros((LANES - n_sel_blk, tq), F32)], axis=0)
        sel_ref[h] = sel_t.T.astype(sel_ref.dtype)


def _cmp_topk(h, cmp_kv, batch, seq, tq):
    hk = NSA_KV_HEADS
    nq = seq // tq
    n_rows = seq // CMP_STRIDE
    n_sel_blk = seq // SEL_LEN
    wq = NSA_HEADS * HEAD_DIM
    assert n_sel_blk <= LANES and tq == LANES
    n_sel = min(SEL_TOPK, n_sel_blk)
    ocmp, sel = pl.pallas_call(
        functools.partial(_cmp_topk_kernel, tq=tq, n_sel_blk=n_sel_blk, n_sel=n_sel),
        grid=(batch, nq),
        in_specs=[
            pl.BlockSpec((tq, wq), lambda b, i: (b * nq + i, EV_QA * LANES // wq)),
            pl.BlockSpec((None, hk, n_rows, HEAD_DIM), lambda b, i: (0, b, 0, 0)),
            pl.BlockSpec((None, hk, n_rows, HEAD_DIM), lambda b, i: (1, b, 0, 0)),
        ],
        out_specs=[
            pl.BlockSpec((tq, wq), lambda b, i: (b * nq + i, 0)),
            pl.BlockSpec((hk, tq, LANES), lambda b, i: (b, i, 0)),
        ],
        out_shape=[
            jax.ShapeDtypeStruct((batch * seq, wq), BF16),
            jax.ShapeDtypeStruct((batch * hk, seq, LANES), BF16),
        ],
        compiler_params=_cparams(("arbitrary", "arbitrary")),
        name="nsa_cmp_topk",
    )(h, cmp_kv, cmp_kv)
    return ocmp, sel.reshape(batch * hk * seq, LANES)


def _lane_tile(x, n):
    return x if n == 1 else jnp.concatenate([x] * n, axis=1)


def _softmax_init(m_ref, l_ref):
    m_ref[...] = jnp.full(m_ref.shape, NEG, F32)
    l_ref[...] = jnp.zeros(l_ref.shape, F32)


def _softmax_rows(s_ref, p_ref, al_ref, m_ref, l_ref, chunk):
    rows, tk = s_ref.shape
    for r in range(rows // chunk):
        sl = slice(r * chunk, (r + 1) * chunk)
        s = s_ref[sl, :]
        m_prev = m_ref[sl, :]
        m_new = jnp.maximum(m_prev, jnp.max(s, axis=1, keepdims=True))
        alpha = jnp.exp2(m_prev - m_new)
        p = jnp.exp2(s - _lane_tile(m_new, tk // LANES))
        l_ref[sl, :] = alpha * l_ref[sl, :] + jnp.sum(p, axis=1, keepdims=True)
        m_ref[sl, :] = m_new
        al_ref[sl, :] = alpha
        p_ref[sl, :] = p.astype(BF16)


def _two_slot_pipeline(n_steps, produce, consume):
    produce(0, 0, True)

    def body(k, _):
        for parity in range(2):
            @pl.when(k % 2 == parity)
            def _(parity=parity):
                produce(1 - parity, k + 1, False)
                consume(parity, k)
        return 0

    lax.fori_loop(0, n_steps - 1, body, 0)
    last = n_steps - 1
    for parity in range(2):
        @pl.when(last % 2 == parity)
        def _(parity=parity):
            consume(parity, last)


def _carried_pipeline(i, produce_first, produce, consume):
    before = (i * (i + 1) // 2) % 2

    @pl.when(i == 0)
    def _():
        produce_first(0, i)

    def body(k, _):
        for parity in range(2):
            @pl.when((before + k) % 2 == parity)
            def _(parity=parity):
                produce(1 - parity, k + 1)
                consume(parity, k)
        return 0

    lax.fori_loop(0, i, body, 0)
    for parity in range(2):
        @pl.when((before + i) % 2 == parity)
        def _(parity=parity):
            produce_first(1 - parity, i + 1)
            consume(parity, i)


def _nsa_attn_kernel(q_ref, kc_ref, vc_ref, ks_ref, vs_ref, kw_ref, vw_ref, cos_ref, sin_ref, ga_ref,
                     gate0_ref, gate1_ref, o_ref, m_ref, l_ref, al_ref, acc_ref, q4_ref, s_ref, p_ref, sw_ref,
                     pw_ref, ow_ref, ocmp_ref, sel_ref, *, tq, tk, n_sel_blk, n_sel):
    g, hk = NSA_GROUP, NSA_KV_HEADS
    rows = g * tq
    i = pl.program_id(1)
    q0 = i * tq
    gate_refs = (gate0_ref, gate1_ref)
    _cmp_topk_kernel(q_ref, kc_ref, vc_ref, ocmp_ref, sel_ref, tq=tq, n_sel_blk=n_sel_blk, n_sel=n_sel)
    kv_cols = lambda h: slice(h * HEAD_DIM, (h + 1) * HEAD_DIM)
    cos2 = jnp.concatenate([cos_ref[...]] * g, axis=0)
    sin2 = jnp.concatenate([sin_ref[...]] * g, axis=0)
    for h in range(hk):
        q_h = q_ref[:, h * g * HEAD_DIM:(h + 1) * g * HEAD_DIM]
        q4_ref[h] = _rope_f32(_heads_to_rows(q_h, g).astype(F32), cos2, sin2).astype(BF16)

    span = WIN + tq
    w0 = pl.multiple_of(jnp.maximum(q0 - WIN, 0), tq)
    dist = (q0 + lax.broadcasted_iota(jnp.int32, (tq, span), 0)) - (w0 + lax.broadcasted_iota(jnp.int32, (tq, span), 1))
    in_window = (dist >= 0) & (dist < WIN)
    _softmax_init(m_ref, l_ref)
    for h in range(hk):
        s = _dot_nt(q4_ref[h], kw_ref[pl.ds(w0, span), kv_cols(h)])
        sw_ref[h] = jnp.where(in_window[None], s.reshape(g, tq, span), NEG).reshape(rows, span)
    for h in range(hk):
        _softmax_rows(sw_ref.at[h], pw_ref.at[h], al_ref.at[h], m_ref.at[h], l_ref.at[h], SOFTMAX_CHUNK)
    for h in range(hk):
        ow_ref[h] = _dot(pw_ref[h], vw_ref[pl.ds(w0, span), kv_cols(h)]) / l_ref[h]

    _softmax_init(m_ref, l_ref)
    acc_ref[...] = jnp.zeros(acc_ref.shape, F32)

    def produce(slot, j, first):
        del first
        k0 = pl.multiple_of(j * tk, tk)
        blk_row = lax.broadcasted_iota(jnp.int32, (LANES, tk), 0)
        blk_of_lane = lax.broadcasted_iota(jnp.int32, (LANES, tk), 1) >> SEL_SHIFT
        expand = jnp.where(blk_row == j * (tk // SEL_LEN) + blk_of_lane, 1.0, 0.0).astype(BF16)
        causal = k0 + lax.broadcasted_iota(jnp.int32, (tq, tk), 1) <= q0 + lax.broadcasted_iota(jnp.int32, (tq, tk), 0)
        for h in range(hk):
            s = _dot_nt(q4_ref[h], ks_ref[pl.ds(k0, tk), kv_cols(h)])
            picked = _dot(sel_ref[h], expand)
            valid = (picked > 0.5) & causal
            s_ref[slot, h] = jnp.where(valid[None], s.reshape(g, tq, tk), NEG).reshape(rows, tk)

    def consume(slot, j):
        k0 = pl.multiple_of(j * tk, tk)
        for h in range(hk):
            _softmax_rows(s_ref.at[slot, h], p_ref.at[h], al_ref.at[h], m_ref.at[h], l_ref.at[h], SOFTMAX_CHUNK)
        for h in range(hk):
            acc_ref[h] = acc_ref[h] * al_ref[h] + _dot(p_ref[h], vs_ref[pl.ds(k0, tk), kv_cols(h)])

    _two_slot_pipeline((q0 + tq - 1) // tk + 1, produce, consume)

    gt = _sigmoid(ga_ref[...].astype(F32))
    for h in range(hk):
        for a in range(g):
            cols = slice((h * g + a) * HEAD_DIM, (h * g + a + 1) * HEAD_DIM)
            head = slice(a * tq, (a + 1) * tq)
            lane = (h * g + a) * 3
            o_sel = acc_ref[h, head, :] / l_ref[h, head, :]
            mix = (gt[:, lane:lane + 1] * ocmp_ref[:, cols].astype(F32) + gt[:, lane + 1:lane + 2] * o_sel
                   + gt[:, lane + 2:lane + 3] * ow_ref[h, head, :])
            gate = _silu(gate_refs[h][:, a * HEAD_DIM:(a + 1) * HEAD_DIM].astype(F32))
            o_ref[:, cols] = (mix * gate).astype(o_ref.dtype)


def _nsa_attn(h, cmp_kv, cos2, sin2, batch, seq, tq, tk):
    hk, g = NSA_KV_HEADS, NSA_GROUP
    nq = seq // tq
    rows = g * tq
    wq = NSA_HEADS * HEAD_DIM
    wkv = hk * HEAD_DIM
    n_rows = seq // CMP_STRIDE
    n_sel_blk = seq // SEL_LEN
    assert WIN % tq == 0 and tk % SEL_LEN == 0 and seq % tk == 0 and seq >= WIN + tq and tk % tq == 0 and hk == 2
    assert n_sel_blk <= LANES and tq == LANES
    slab = lambda col: pl.BlockSpec((seq, wkv), lambda b, i: (b, col // hk))
    cmp_spec = lambda which: pl.BlockSpec((None, hk, n_rows, HEAD_DIM), lambda b, i: (which, b, 0, 0))
    return pl.pallas_call(
        functools.partial(_nsa_attn_kernel, tq=tq, tk=tk, n_sel_blk=n_sel_blk, n_sel=min(SEL_TOPK, n_sel_blk)),
        grid=(batch, nq),
        in_specs=[
            pl.BlockSpec((tq, wq), lambda b, i: (b * nq + i, EV_QA * LANES // wq)),
            cmp_spec(0), cmp_spec(1),
            slab(EV_KS), slab(EV_VS), slab(EV_KW), slab(EV_VW),
            pl.BlockSpec((tq, LANES), lambda b, i: (i, 0)),
            pl.BlockSpec((tq, LANES), lambda b, i: (i, 0)),
            pl.BlockSpec((tq, LANES), lambda b, i: (b * nq + i, EV_GA)),
            pl.BlockSpec((tq, g * HEAD_DIM), lambda b, i: (b * nq + i, EV_GATE_A // g)),
            pl.BlockSpec((tq, g * HEAD_DIM), lambda b, i: (b * nq + i, EV_GATE_A // g + 1)),
        ],
        out_specs=pl.BlockSpec((tq, wq), lambda b, i: (b * nq + i, 0)),
        out_shape=jax.ShapeDtypeStruct((batch * seq, wq), BF16),
        scratch_shapes=[
            pltpu.VMEM((hk, rows, LANES), F32), pltpu.VMEM((hk, rows, LANES), F32), pltpu.VMEM((hk, rows, LANES), F32),
            pltpu.VMEM((hk, rows, HEAD_DIM), F32), pltpu.VMEM((hk, rows, HEAD_DIM), BF16),
            pltpu.VMEM((2, hk, rows, tk), F32), pltpu.VMEM((hk, rows, tk), BF16),
            pltpu.VMEM((hk, rows, WIN + tq), F32), pltpu.VMEM((hk, rows, WIN + tq), BF16),
            pltpu.VMEM((hk, rows, HEAD_DIM), F32),
            pltpu.VMEM((tq, wq), BF16), pltpu.VMEM((hk, tq, LANES), BF16),
        ],
        compiler_params=_cparams(("arbitrary", "arbitrary")),
        name="nsa_attn",
    )(h, cmp_kv, cmp_kv, h, h, h, h, cos2, sin2, h, h, h)


def _sb_kernel(q_ref, k_ref, v_ref, gate_ref, tri_ref, o_ref, later_ref, acc_ref, z_ref, zs_ref, sp16_ref, sp0_ref,
               after_ref, a_ref, *, t):
    i = pl.program_id(2)
    n_q = pl.num_programs(2)
    chunk = SOFTMAX_CHUNK
    nh = acc_ref.shape[0]
    tb = tri_ref.shape[0]
    nb = t // tb
    later_ref[...] = jnp.zeros(later_ref.shape, F32)
    acc_ref[...] = jnp.zeros(acc_ref.shape, F32)
    head_cols = lambda h: slice(h * HEAD_DIM, (h + 1) * HEAD_DIM)

    def logits(slot, qi, kj, diagonal):
        q0 = pl.multiple_of(qi * t, t)
        k0 = pl.multiple_of(kj * t, t)
        for h in range(nh):
            z = _dot_nt(q_ref[pl.ds(q0, t), head_cols(h)], k_ref[pl.ds(k0, t), head_cols(h)])
            if diagonal:
                strict = lax.broadcasted_iota(jnp.int32, (t, t), 1) < lax.broadcasted_iota(jnp.int32, (t, t), 0)
                z = jnp.where(strict, z, NEG)
            z_ref[slot, h] = z

    def produce_first(slot, qi):
        qi = jnp.minimum(qi, n_q - 1)
        logits(slot, qi, qi, True)

    def produce(slot, k):
        logits(slot, i, i - k, False)

    def consume(slot, k):
        k0 = pl.multiple_of((i - k) * t, t)
        for h in range(nh):
            for r in range(t // chunk):
                sl = slice(r * chunk, (r + 1) * chunk)
                z = z_ref[slot, h, sl, :]
                sp = jnp.maximum(z, 0.0) + jnp.log(1.0 + jnp.exp2(jnp.minimum(z, -z))) * LOG2E
                zs_ref[h, sl, :] = z - sp
                sp16_ref[h, sl, :] = sp.astype(BF16)
                sp0_ref[h, sl, :] = jnp.concatenate([sp[:, b * tb:b * tb + LANES] for b in range(nb)], axis=1)
        for h in range(nh):
            for b in range(nb):
                bc = slice(b * tb, (b + 1) * tb)
                after_ref[h, :, bc] = _dot(sp16_ref[h, :, bc], tri_ref[...])
        for h in range(nh):
            for r in range(t // chunk):
                sl = slice(r * chunk, (r + 1) * chunk)
                run = later_ref[h, sl, :]
                for b in reversed(range(nb)):
                    bc = slice(b * tb, (b + 1) * tb)
                    after = after_ref[h, sl, bc]
                    a_ref[h, sl, bc] = jnp.exp2(zs_ref[h, sl, bc] - after - _lane_tile(run, tb // LANES)).astype(BF16)
                    run = run + (after[:, 0:1] + sp0_ref[h, sl, b * LANES:b * LANES + 1])
                later_ref[h, sl, :] = run
            acc_ref[h] += _dot(a_ref[h], v_ref[pl.ds(k0, t), head_cols(h)])

    _carried_pipeline(i, produce_first, produce, consume)
    for h in range(nh):
        o_ref[:, head_cols(h)] = (acc_ref[h] * _silu(gate_ref[:, head_cols(h)].astype(F32))).astype(o_ref.dtype)


def _stick_breaking(h, batch, seq, t, nh, tb):
    nq = seq // t
    w = nh * HEAD_DIM
    assert SB_HEADS % nh == 0 and t % tb == 0
    tri = (lax.broadcasted_iota(jnp.int32, (tb, tb), 0) > lax.broadcasted_iota(jnp.int32, (tb, tb), 1)).astype(BF16)
    col = lambda first_block: first_block // nh
    return pl.pallas_call(
        functools.partial(_sb_kernel, t=t),
        grid=(batch, SB_HEADS // nh, nq),
        in_specs=[
            pl.BlockSpec((seq, w), lambda b, h_, i: (b, col(EV_QB) + h_)),
            pl.BlockSpec((seq, w), lambda b, h_, i: (b, col(EV_KB) + h_)),
            pl.BlockSpec((seq, w), lambda b, h_, i: (b, col(EV_VB) + h_)),
            pl.BlockSpec((t, w), lambda b, h_, i: (b * nq + i, col(EV_GATE_B) + h_)),
            pl.BlockSpec((tb, tb), lambda b, h_, i: (0, 0)),
        ],
        out_specs=pl.BlockSpec((t, w), lambda b, h_, i: (b * nq + i, h_)),
        out_shape=jax.ShapeDtypeStruct((batch * seq, SB_HEADS * HEAD_DIM), BF16),
        scratch_shapes=[
            pltpu.VMEM((nh, t, LANES), F32), pltpu.VMEM((nh, t, HEAD_DIM), F32), pltpu.VMEM((2, nh, t, t), F32),
            pltpu.VMEM((nh, t, t), F32), pltpu.VMEM((nh, t, t), BF16), pltpu.VMEM((nh, t, (t // tb) * LANES), F32),
            pltpu.VMEM((nh, t, t), F32), pltpu.VMEM((nh, t, t), BF16),
        ],
        compiler_params=_cparams(("arbitrary", "arbitrary", "arbitrary")),
        name="stick_breaking",
    )(h, h, h, h, tri)


def _diff_kernel(q_ref, k_ref, v_ref, gate_ref, gn_ref, lq1_ref, lk1_ref, lq2_ref, lk2_ref, o_ref,
                 m_ref, l_ref, acc_ref, s_ref, p_ref, al_ref, *, t, lambda_init):
    d = HEAD_DIM
    i = pl.program_id(2)
    n_q = pl.num_programs(2)
    n_str = m_ref.shape[0]
    _softmax_init(m_ref, l_ref)
    acc_ref[...] = jnp.zeros(acc_ref.shape, F32)

    def scores(slot, qi, kj, diagonal):
        q0 = pl.multiple_of(qi * t, t)
        k0 = pl.multiple_of(kj * t, t)
        for u in range(n_str):
            s = _dot_nt(q_ref[pl.ds(q0, t), u * d:(u + 1) * d], k_ref[pl.ds(k0, t), u * d:(u + 1) * d])
            if diagonal:
                causal = lax.broadcasted_iota(jnp.int32, (t, t), 1) <= lax.broadcasted_iota(jnp.int32, (t, t), 0)
                s = jnp.where(causal, s, NEG)
            s_ref[slot, u] = s

    def produce_first(slot, qi):
        qi = jnp.minimum(qi, n_q - 1)
        scores(slot, qi, qi, True)

    def produce(slot, k):
        scores(slot, i, k - 1, False)

    def consume(slot, k):
        k0 = pl.multiple_of(jnp.where(k == 0, i, k - 1) * t, t)
        for u in range(n_str):
            _softmax_rows(s_ref.at[slot, u], p_ref.at[u], al_ref.at[u], m_ref.at[u], l_ref.at[u], SOFTMAX_CHUNK)
        for u in range(n_str):
            vv = v_ref[pl.ds(k0, t), (u // 2) * 2 * d:(u // 2 + 1) * 2 * d]
            acc_ref[u] = acc_ref[u] * _lane_tile(al_ref[u], 2) + _dot(p_ref[u], vv)

    _carried_pipeline(i, produce_first, produce, consume)
    lam = (jnp.exp(jnp.sum(lq1_ref[...] * lk1_ref[...], axis=-1, keepdims=True))
           - jnp.exp(jnp.sum(lq2_ref[...] * lk2_ref[...], axis=-1, keepdims=True)) + lambda_init)
    for hh in range(n_str // 2):
        cols = slice(hh * 2 * d, (hh + 1) * 2 * d)
        inv = [1.0 / _lane_tile(l_ref[2 * hh + c], 2) for c in range(2)]
        o = acc_ref[2 * hh] * inv[0] - lam * (acc_ref[2 * hh + 1] * inv[1])
        o = o * lax.rsqrt(jnp.mean(o * o, axis=-1, keepdims=True) + RMS_EPS)
        o = o * gn_ref[hh:hh + 1, :] * (1.0 - lambda_init)
        o_ref[:, cols] = (o * _silu(gate_ref[:, cols].astype(F32))).astype(o_ref.dtype)


def _diff_attn(h, gn_g, lq1, lk1, lq2, lk2, lambda_init, batch, seq, t, hps):
    nh = DIFF_HEADS
    nq = seq // t
    w = hps * 2 * HEAD_DIM
    ng = nh // hps
    assert nh % hps == 0
    vec = pl.BlockSpec((1, HEAD_DIM), lambda b, h_, i: (0, 0))
    r = lambda a: a.reshape(1, HEAD_DIM)
    return pl.pallas_call(
        functools.partial(_diff_kernel, t=t, lambda_init=lambda_init),
        grid=(batch, ng, nq),
        in_specs=[
            pl.BlockSpec((seq, w), lambda b, h_, i: (b, h_)),
            pl.BlockSpec((seq, w), lambda b, h_, i: (b, ng + h_)),
            pl.BlockSpec((seq, w), lambda b, h_, i: (b, 2 * ng + h_)),
            pl.BlockSpec((t, w), lambda b, h_, i: (b * nq + i, 3 * ng + h_)),
            pl.BlockSpec((None, hps, 2 * HEAD_DIM), lambda b, h_, i: (h_, 0, 0)),
            vec, vec, vec, vec,
        ],
        out_specs=pl.BlockSpec((t, w), lambda b, h_, i: (b * nq + i, h_)),
        out_shape=jax.ShapeDtypeStruct((batch * seq, nh * 2 * HEAD_DIM), BF16),
        scratch_shapes=[
            pltpu.VMEM((2 * hps, t, LANES), F32), pltpu.VMEM((2 * hps, t, LANES), F32),
            pltpu.VMEM((2 * hps, t, 2 * HEAD_DIM), F32), pltpu.VMEM((2, 2 * hps, t, t), F32),
            pltpu.VMEM((2 * hps, t, t), BF16), pltpu.VMEM((2 * hps, t, LANES), F32),
        ],
        compiler_params=_cparams(("arbitrary", "arbitrary", "arbitrary")),
        name="diff_attn",
    )(h, h, h, h, gn_g.reshape(ng, hps, 2 * HEAD_DIM), r(lq1), r(lk1), r(lq2), r(lk2))


def _rope_tables(seq):
    pos = jnp.arange(seq, dtype=F32)
    inv = ROPE_THETA ** (-jnp.arange(0, HEAD_DIM, 2, dtype=F32) / HEAD_DIM)
    ang = pos[:, None] * inv[None, :]
    cos, sin = jnp.cos(ang), jnp.sin(ang)
    return jnp.concatenate([cos, cos], axis=-1), jnp.concatenate([-sin, sin], axis=-1)


def _even_weights_nk(w_in, tn):
    tail0 = EV_GA * LANES + 3 * NSA_HEADS
    w_t = w_in.T
    n_head, n_tail = EV_GATE_A * LANES // tn, (w_t.shape[0] - tail0) // tn
    assert EV_GATE_A * LANES % tn == 0 and (n_head + n_tail) * tn == EV_WIDTH
    return [w_t, w_t[tail0:].astype(BF16)], [(0, t) for t in range(n_head)] + [(1, t) for t in range(n_tail)]


def _even_tile_modes(tn):
    modes = [(False, False)] * (EV_WIDTH // LANES)
    for c in range(EV_QA, EV_QA + NSA_HEADS):
        modes[c] = (False, True)
    for c in list(range(EV_KS, EV_KS + NSA_KV_HEADS)) + list(range(EV_KW, EV_KW + NSA_KV_HEADS)):
        modes[c] = (True, False)
    for c in range(EV_QB, EV_QB + SB_HEADS):
        modes[c] = (False, True)
    per = tn // LANES
    return tuple(tuple(modes[i * per:(i + 1) * per]) for i in range(len(modes) // per))


def _odd_tile_modes(tn):
    nh = 2 * DIFF_HEADS
    modes = [(True, True)] * nh + [(True, False)] * nh + [(False, False)] * (2 * nh)
    per = tn // LANES
    return tuple(tuple(modes[i * per:(i + 1) * per]) for i in range(len(modes) // per))


TM_PROJ, TN_PROJ = 1024, 1024
TM_OUT = 512
TQ_NSA = 128
TQ_NSA_ATTN = 128
TK_NSA = 512
T_SB = 512
SB_HEADS_PER_STEP = 2
SB_TRI_BLOCK = 256
T_DIFF = 512
DIFF_HEADS_PER_STEP = 1


def _even_layer(x2, batch, seq, w_in, pe_k, pe_v, w1k, w2k, w1v, w2v, w_out, ln_g, ln_b, cos2, sin2):
    d = HEAD_DIM
    tm = min(TM_PROJ, seq)
    ws, tile_src = _even_weights_nk(w_in, TN_PROJ)
    h = _inproj(x2, ws, tile_src, True, cos2, sin2, _even_tile_modes(TN_PROJ), tm, TN_PROJ)
    pe = jnp.stack([pe_k, pe_v])
    w1 = jnp.stack([w1k, w1v]).reshape(2, CMP_LEN, d, d).astype(BF16)
    w2 = jnp.stack([w2k, w2v]).astype(BF16)
    cmp_kv = _compress(h, pe, w1, w2, batch, seq)
    o_a = _nsa_attn(h, cmp_kv, cos2, sin2, batch, seq, TQ_NSA, min(TK_NSA, seq))
    o_b = _stick_breaking(h, batch, seq, min(T_SB, seq), SB_HEADS_PER_STEP, SB_TRI_BLOCK)
    return _outproj_ln([o_a, o_b], w_out.astype(BF16), x2, ln_g, ln_b, TM_OUT)


def _odd_layer(x2, batch, seq, w_in, lq1, lk1, lq2, lk2, gn_g, w_out, ln_g, ln_b, lambda_init, cos2, sin2):
    tm = min(TM_PROJ, seq)
    tile_src = [(0, t) for t in range(w_in.shape[1] // TN_PROJ)]
    h = _inproj(x2, [w_in.astype(BF16)], tile_src, False, cos2, sin2, _odd_tile_modes(TN_PROJ), tm, TN_PROJ)
    o = _diff_attn(h, gn_g, lq1, lk1, lq2, lk2, lambda_init, batch, seq, min(T_DIFF, seq), DIFF_HEADS_PER_STEP)
    return _outproj_ln([o], w_out.astype(BF16), x2, ln_g, ln_b, TM_OUT)


def kernel(x, ev_w_in, ev_pe_k, ev_pe_v, ev_w1_k, ev_w2_k, ev_w1_v, ev_w2_v, ev_w_out, ev_ln_g, ev_ln_b,
           od_w_in, od_lq1, od_lk1, od_lq2, od_lk2, od_gn_g, od_w_out, od_ln_g, od_ln_b):
    batch, seq, dm = x.shape
    cos2, sin2 = _rope_tables(seq)
    x2 = x.reshape(batch * seq, dm)
    for layer in range(DEPTH):
        i = layer // 2
        if layer % 2 == 0:
            x2 = _even_layer(x2, batch, seq, ev_w_in[i], ev_pe_k[i], ev_pe_v[i], ev_w1_k[i], ev_w2_k[i],
                             ev_w1_v[i], ev_w2_v[i], ev_w_out[i], ev_ln_g[i], ev_ln_b[i], cos2, sin2)
        else:
            lambda_init = 0.8 - 0.6 * math.exp(-0.3 * layer)
            x2 = _odd_layer(x2, batch, seq, od_w_in[i], od_lq1[i], od_lk1[i], od_lq2[i], od_lk2[i], od_gn_g[i],
                            od_w_out[i], od_ln_g[i], od_ln_b[i], lambda_init, cos2, sin2)
    return x2.reshape(batch, seq, dm)
```

```python
import functools
import math

import jax
import jax.numpy as jnp
from jax import lax
from jax.experimental import pallas as pl
from jax.experimental.pallas import tpu as pltpu

F32 = jnp.float32
BF16 = jnp.bfloat16

HEAD_DIM = 128
ROPE_THETA = 10000.0
NSA_HEADS = 8
NSA_KV_HEADS = 2
NSA_GROUP = NSA_HEADS // NSA_KV_HEADS
CMP_LEN = 32
CMP_STRIDE = 16
SEL_LEN = 64
SEL_SHIFT = SEL_LEN.bit_length() - 1
SEL_TOPK = 16
WIN = 512
SEL_FORCE = 1.0e4
SB_HEADS = 8
DIFF_HEADS = 8
DEPTH = 2
DEEPNORM_ALPHA = (2.0 * DEPTH) ** 0.25
LN_EPS = 1e-5
RMS_EPS = 1e-5
LOG2E = math.log2(math.e)
SCALE = HEAD_DIM ** -0.5 * LOG2E

LANES = 128
NEG = -1.0e30
OUT_ROW_CHUNK = 256
SOFTMAX_CHUNK = 64
VMEM_LIMIT = 56 * 1024 * 1024

MXU_TILE = 256
EV_QA, EV_KC, EV_VC, EV_KS, EV_VS, EV_KW, EV_VW, EV_GA = 0, 8, 10, 12, 14, 16, 18, 20
EV_GATE_A, EV_QB, EV_KB, EV_VB, EV_GATE_B = 24, 32, 40, 48, 56
EV_WIDTH = 64 * LANES


def _cparams(sem):
    return pltpu.CompilerParams(dimension_semantics=sem, vmem_limit_bytes=VMEM_LIMIT)


def _dot_nt(a, b):
    return lax.dot_general(a, b, (((1,), (1,)), ((), ())), preferred_element_type=F32)


def _dot(a, b):
    return jnp.dot(a, b, preferred_element_type=F32)


def _silu(x):
    return x * (1.0 / (1.0 + jnp.exp(-x)))


def _sigmoid(x):
    return 1.0 / (1.0 + jnp.exp(-x))


def _rope_f32(x, cos2, sin2):
    return x * cos2 + pltpu.roll(x, HEAD_DIM // 2, axis=1) * sin2


def _inproj_kernel(*refs, tile_modes, tile_src, w_is_nk):
    n_w = len(refs) - 5
    x_ref, w_refs, (cos_ref, sin_ref, o_ref, xb_ref) = refs[0], refs[1:1 + n_w], refs[1 + n_w:]
    n = pl.program_id(1)

    @pl.when(n == 0)
    def _():
        xb_ref[...] = x_ref[...].astype(BF16)

    variants = {}
    for tile, pat in enumerate(tile_modes):
        variants.setdefault((tile_src[tile][0], pat), []).append(tile)
    per_chunk = MXU_TILE // LANES
    for (w_idx, pat), tiles in variants.items():
        cond = functools.reduce(jnp.logical_or, [n == t for t in tiles])

        @pl.when(cond)
        def _(pat=pat, w_ref=w_refs[w_idx]):
            for cc in range(len(pat) // per_chunk):
                chunk = slice(cc * MXU_TILE, (cc + 1) * MXU_TILE)
                wc = (w_ref[chunk, :] if w_is_nk else w_ref[:, chunk]).astype(BF16)
                acc = _dot_nt(xb_ref[...], wc) if w_is_nk else _dot(xb_ref[...], wc)
                for c2 in range(per_chunk):
                    c = cc * per_chunk + c2
                    rope, scale = pat[c]
                    y = acc[:, c2 * LANES:(c2 + 1) * LANES]
                    if rope:
                        y = _rope_f32(y, cos_ref[...], sin_ref[...])
                    if scale:
                        y = y * SCALE
                    o_ref[:, c * LANES:(c + 1) * LANES] = y.astype(o_ref.dtype)


def _inproj(x, ws, tile_src, w_is_nk, cos2, sin2, tile_modes, tm, tn):
    m, k = x.shape
    n_tiles = len(tile_src)
    s = cos2.shape[0]
    assert m % tm == 0 and s % tm == 0 and len(tile_modes) == n_tiles
    pos_tiles = s // tm

    def w_spec(a):
        table, last = [], [blk for (src, blk) in tile_src if src == a][0]
        for src, blk in tile_src:
            last = blk if src == a else last
            table.append(last)

        def block(j):
            blk = jnp.int32(table[0])
            for t in range(1, n_tiles):
                blk = jnp.where(j == t, jnp.int32(table[t]), blk)
            return blk

        if w_is_nk:
            return pl.BlockSpec((tn, k), lambda i, j: (block(j), 0))
        return pl.BlockSpec((k, tn), lambda i, j: (0, block(j)))

    return pl.pallas_call(
        functools.partial(_inproj_kernel, tile_modes=tile_modes, tile_src=tuple(tile_src), w_is_nk=w_is_nk),
        grid=(m // tm, n_tiles),
        in_specs=[pl.BlockSpec((tm, k), lambda i, j: (i, 0))] + [w_spec(a) for a in range(len(ws))] + [
            pl.BlockSpec((tm, LANES), lambda i, j: (i % pos_tiles, 0)),
            pl.BlockSpec((tm, LANES), lambda i, j: (i % pos_tiles, 0)),
        ],
        out_specs=pl.BlockSpec((tm, tn), lambda i, j: (i, j)),
        out_shape=jax.ShapeDtypeStruct((m, n_tiles * tn), BF16),
        scratch_shapes=[pltpu.VMEM((tm, k), BF16)],
        compiler_params=_cparams(("arbitrary", "arbitrary")),
        name="inproj",
    )(x, *ws, cos2, sin2)


def _outproj_ln_kernel(*refs, n_parts, n_side):
    o_refs = refs[:n_parts]
    w_ref, x_ref, g_ref, b_ref = refs[n_parts:n_parts + 4]
    side_in = refs[n_parts + 4:n_parts + 4 + n_side]
    out_ref = refs[n_parts + 4 + n_side]
    side_out = refs[n_parts + 5 + n_side:]
    kp = w_ref.shape[0] // n_parts
    for r in range(x_ref.shape[0] // OUT_ROW_CHUNK):
        rows = slice(r * OUT_ROW_CHUNK, (r + 1) * OUT_ROW_CHUNK)
        y = None
        for p in range(n_parts):
            part = _dot(o_refs[p][rows, :], w_ref[p * kp:(p + 1) * kp, :])
            y = part if y is None else y + part
        z = DEEPNORM_ALPHA * x_ref[rows, :] + y
        mu = jnp.mean(z, axis=-1, keepdims=True)
        zc = z - mu
        var = jnp.mean(zc * zc, axis=-1, keepdims=True)
        out_ref[rows, :] = zc * lax.rsqrt(var + LN_EPS) * g_ref[...] + b_ref[...]
    for src, dst in zip(side_in, side_out):
        dst[...] = src[...].astype(dst.dtype)


def _outproj_ln(parts, w, x, g, b, tm, side=()):
    m, d = x.shape
    n_parts = len(parts)
    kp = w.shape[0] // n_parts
    steps = m // tm
    assert all(a.ndim == 2 and a.shape[0] % (8 * steps) == 0 for a in side)
    side_spec = [pl.BlockSpec((a.shape[0] // steps, a.shape[1]), lambda i: (i, 0)) for a in side]
    out = pl.pallas_call(
        functools.partial(_outproj_ln_kernel, n_parts=n_parts, n_side=len(side)),
        grid=(steps,),
        in_specs=[pl.BlockSpec((tm, kp), lambda i: (i, 0)) for _ in parts] + [
            pl.BlockSpec(w.shape, lambda i: (0, 0)),
            pl.BlockSpec((tm, d), lambda i: (i, 0)),
            pl.BlockSpec((1, d), lambda i: (0, 0)),
            pl.BlockSpec((1, d), lambda i: (0, 0)),
        ] + side_spec,
        out_specs=[pl.BlockSpec((tm, d), lambda i: (i, 0))] + side_spec,
        out_shape=[jax.ShapeDtypeStruct((m, d), F32)] + [jax.ShapeDtypeStruct(a.shape, BF16) for a in side],
        compiler_params=_cparams(("arbitrary",)),
        name="outproj_ln",
    )(*parts, w, x, g.reshape(1, d), b.reshape(1, d), *side)
    return out[0], tuple(out[1:])


def _compress_kernel(kv_ref, pe_ref, w1_ref, w2_ref, o_ref, pad_ref, *, seq):
    n_rows = seq // CMP_STRIDE
    pad_ref[0:seq, :] = kv_ref[...].astype(F32)
    pad_ref[seq:seq + CMP_STRIDE, :] = jnp.zeros((CMP_STRIDE, HEAD_DIM), F32)
    pre = jnp.zeros((n_rows, HEAD_DIM), F32)
    for l in range(CMP_LEN):
        rows = pad_ref[pl.ds(l, n_rows, stride=CMP_STRIDE), :] + pe_ref[l:l + 1, :]
        pre = pre + _dot(rows.astype(BF16), w1_ref[l])
    o_ref[...] = _dot(_silu(pre).astype(BF16), w2_ref[...]).astype(o_ref.dtype)


def _compress(h, pe, w1, w2, batch, seq):
    n_rows = seq // CMP_STRIDE
    hk = NSA_KV_HEADS
    return pl.pallas_call(
        functools.partial(_compress_kernel, seq=seq),
        grid=(2, batch, hk),
        in_specs=[
            pl.BlockSpec((seq, HEAD_DIM), lambda c, b, h: (b, EV_KC + c * hk + h)),
            pl.BlockSpec((None, CMP_LEN, HEAD_DIM), lambda c, b, h: (c, 0, 0)),
            pl.BlockSpec((None, CMP_LEN, HEAD_DIM, HEAD_DIM), lambda c, b, h: (c, 0, 0, 0)),
            pl.BlockSpec((None, HEAD_DIM, HEAD_DIM), lambda c, b, h: (c, 0, 0)),
        ],
        out_specs=pl.BlockSpec((None, None, n_rows, HEAD_DIM), lambda c, b, h: (c, b * hk + h, 0, 0)),
        out_shape=jax.ShapeDtypeStruct((2, batch * hk, n_rows, HEAD_DIM), BF16),
        scratch_shapes=[pltpu.VMEM((seq + CMP_STRIDE, HEAD_DIM), F32)],
        compiler_params=_cparams(("arbitrary", "arbitrary", "arbitrary")),
        name="nsa_compress",
    )(h, pe, w1, w2)


def _heads_to_rows(x, g):
    return jnp.concatenate([x[:, i * HEAD_DIM:(i + 1) * HEAD_DIM] for i in range(g)], axis=0)


def _rows_to_heads(x, g):
    r = x.shape[0] // g
    return jnp.concatenate([x[i * r:(i + 1) * r, :] for i in range(g)], axis=1)


def _cmp_topk_kernel(q_ref, kc_ref, vc_ref, ocmp_ref, sel_ref, *, tq, n_sel_blk, n_sel):
    g, hk = NSA_GROUP, NSA_KV_HEADS
    i = pl.program_id(1)
    q0 = i * tq
    n_rows = kc_ref.shape[1]
    t_col = q0 + lax.broadcasted_iota(jnp.int32, (tq, n_rows), 0)
    blk_end = lax.broadcasted_iota(jnp.int32, (tq, n_rows), 1) * CMP_STRIDE + (CMP_LEN - 1)
    vis = (blk_end <= t_col)[None]
    jj = lax.broadcasted_iota(jnp.int32, (n_sel_blk, n_rows), 0) * SEL_LEN
    cs = lax.broadcasted_iota(jnp.int32, (n_sel_blk, n_rows), 1) * CMP_STRIDE
    ov_t = jnp.where((cs < jj + SEL_LEN) & (cs + CMP_LEN > jj), 1.0, 0.0).astype(BF16)
    blk = lax.broadcasted_iota(jnp.int32, (n_sel_blk, tq), 0)
    j_cur = (q0 + lax.broadcasted_iota(jnp.int32, (n_sel_blk, tq), 1)) >> SEL_SHIFT
    forced = (blk == 0) | (blk == j_cur) | (blk == j_cur - 1)

    scores = []
    for h in range(hk):
        q4 = _heads_to_rows(q_ref[:, h * g * HEAD_DIM:(h + 1) * g * HEAD_DIM], g)
        sc = _dot_nt(q4, kc_ref[h]).reshape(g, tq, n_rows)
        sm = jnp.where(vis, sc, NEG)
        mx = jnp.max(sm, axis=-1, keepdims=True)
        e = jnp.where(vis, jnp.exp2(sm - mx), 0.0)
        den = jnp.sum(e, axis=-1, keepdims=True)
        p = e / jnp.where(den > 0.0, den, 1.0)
        o = _dot(p.reshape(g * tq, n_rows).astype(BF16), vc_ref[h])
        ocmp_ref[:, h * g * HEAD_DIM:(h + 1) * g * HEAD_DIM] = _rows_to_heads(o, g).astype(ocmp_ref.dtype)

        p_sum = jnp.sum(p, axis=0)
        p_hi = p_sum.astype(BF16)
        p_lo = (p_sum - p_hi.astype(F32)).astype(BF16)
        imp_t = _dot_nt(ov_t, p_hi) + _dot_nt(ov_t, p_lo)
        scores.append(jnp.where(forced, SEL_FORCE, jnp.where(blk <= j_cur, imp_t, -SEL_FORCE)))

    ranks = [jnp.zeros((n_sel_blk, tq), F32) for _ in range(hk)]
    for r in range(n_sel_blk):
        for h in range(hk):
            row = scores[h][r:r + 1, :]
            ahead = (row > scores[h]) | ((row == scores[h]) & (blk > r))
            ranks[h] = ranks[h] + jnp.where(ahead, 1.0, 0.0)
    for h in range(hk):
        sel_t = jnp.where(ranks[h] < float(n_sel), 1.0, 0.0)
        if n_sel_blk < LANES:
            sel_t = jnp.concatenate([sel_t, jnp.zeros((LANES - n_sel_blk, tq), F32)], axis=0)
        sel_ref[h] = sel_t.T.astype(sel_ref.dtype)


def _lane_tile(x, n):
    return x if n == 1 else jnp.concatenate([x] * n, axis=1)


def _softmax_init(m_ref, l_ref):
    m_ref[...] = jnp.full(m_ref.shape, NEG, F32)
    l_ref[...] = jnp.zeros(l_ref.shape, F32)


def _softmax_rows(s_ref, p_ref, al_ref, m_ref, l_ref, chunk):
    rows, tk = s_ref.shape
    for r in range(rows // chunk):
        sl = slice(r * chunk, (r + 1) * chunk)
        s = s_ref[sl, :]
        m_prev = m_ref[sl, :]
        m_new = jnp.maximum(m_prev, jnp.max(s, axis=1, keepdims=True))
        alpha = jnp.exp2(m_prev - m_new)
        p = jnp.exp2(s - _lane_tile(m_new, tk // LANES))
        l_ref[sl, :] = alpha * l_ref[sl, :] + jnp.sum(p, axis=1, keepdims=True)
        m_ref[sl, :] = m_new
        al_ref[sl, :] = alpha
        p_ref[sl, :] = p.astype(BF16)


def _two_slot_pipeline(n_steps, produce, consume):
    produce(0, 0, True)

    def body(k, _):
        for parity in range(2):
            @pl.when(k % 2 == parity)
            def _(parity=parity):
                produce(1 - parity, k + 1, False)
                consume(parity, k)
        return 0

    lax.fori_loop(0, n_steps - 1, body, 0)
    last = n_steps - 1
    for parity in range(2):
        @pl.when(last % 2 == parity)
        def _(parity=parity):
            consume(parity, last)


def _carried_pipeline(i, steps_per_tile, produce_first, produce, consume):
    before, n_steps = jnp.int32(0), jnp.int32(steps_per_tile[0])
    for j in range(1, len(steps_per_tile)):
        before = before + jnp.where(j <= i, steps_per_tile[j - 1], 0)
        n_steps = jnp.where(j == i, steps_per_tile[j], n_steps)
    last = n_steps - 1

    @pl.when(i == 0)
    def _():
        produce_first(0, i)

    def body(k, _):
        for parity in range(2):
            @pl.when((before + k) % 2 == parity)
            def _(parity=parity):
                produce(1 - parity, k + 1)
                consume(parity, k)
        return 0

    lax.fori_loop(0, last, body, 0)
    for parity in range(2):
        @pl.when((before + last) % 2 == parity)
        def _(parity=parity):
            produce_first(1 - parity, i + 1)
            consume(parity, last)


def _nsa_attn_kernel(q_ref, kc_ref, vc_ref, ks_ref, vs_ref, kw_ref, vw_ref, cos_ref, sin_ref, ga_ref,
                     gate0_ref, gate1_ref, o_ref, m_ref, l_ref, al_ref, acc_ref, q4_ref, s_ref, p_ref, sw_ref,
                     pw_ref, ow_ref, ocmp_ref, sel_ref, *, tq, tk, n_sel_blk, n_sel):
    g, hk = NSA_GROUP, NSA_KV_HEADS
    rows = g * tq
    i = pl.program_id(1)
    q0 = i * tq
    gate_refs = (gate0_ref, gate1_ref)
    _cmp_topk_kernel(q_ref, kc_ref, vc_ref, ocmp_ref, sel_ref, tq=tq, n_sel_blk=n_sel_blk, n_sel=n_sel)
    kv_cols = lambda h: slice(h * HEAD_DIM, (h + 1) * HEAD_DIM)
    cos2 = jnp.concatenate([cos_ref[...]] * g, axis=0)
    sin2 = jnp.concatenate([sin_ref[...]] * g, axis=0)
    for h in range(hk):
        q_h = q_ref[:, h * g * HEAD_DIM:(h + 1) * g * HEAD_DIM]
        q4_ref[h] = _rope_f32(_heads_to_rows(q_h, g).astype(F32), cos2, sin2).astype(BF16)

    span = WIN + tq
    w0 = pl.multiple_of(jnp.maximum(q0 - WIN, 0), tq)
    dist = (q0 + lax.broadcasted_iota(jnp.int32, (tq, span), 0)) - (w0 + lax.broadcasted_iota(jnp.int32, (tq, span), 1))
    in_window = (dist >= 0) & (dist < WIN)
    _softmax_init(m_ref, l_ref)
    for h in range(hk):
        s = _dot_nt(q4_ref[h], kw_ref[pl.ds(w0, span), kv_cols(h)])
        sw_ref[h] = jnp.where(in_window[None], s.reshape(g, tq, span), NEG).reshape(rows, span)
    for h in range(hk):
        _softmax_rows(sw_ref.at[h], pw_ref.at[h], al_ref.at[h], m_ref.at[h], l_ref.at[h], SOFTMAX_CHUNK)
    for h in range(hk):
        ow_ref[h] = _dot(pw_ref[h], vw_ref[pl.ds(w0, span), kv_cols(h)]) / l_ref[h]

    _softmax_init(m_ref, l_ref)
    acc_ref[...] = jnp.zeros(acc_ref.shape, F32)

    def produce(slot, j, first):
        del first
        k0 = pl.multiple_of(j * tk, tk)
        blk_row = lax.broadcasted_iota(jnp.int32, (LANES, tk), 0)
        blk_of_lane = lax.broadcasted_iota(jnp.int32, (LANES, tk), 1) >> SEL_SHIFT
        expand = jnp.where(blk_row == j * (tk // SEL_LEN) + blk_of_lane, 1.0, 0.0).astype(BF16)
        causal = k0 + lax.broadcasted_iota(jnp.int32, (tq, tk), 1) <= q0 + lax.broadcasted_iota(jnp.int32, (tq, tk), 0)
        for h in range(hk):
            s = _dot_nt(q4_ref[h], ks_ref[pl.ds(k0, tk), kv_cols(h)])
            picked = _dot(sel_ref[h], expand)
            valid = (picked > 0.5) & causal
            s_ref[slot, h] = jnp.where(valid[None], s.reshape(g, tq, tk), NEG).reshape(rows, tk)

    def consume(slot, j):
        k0 = pl.multiple_of(j * tk, tk)
        for h in range(hk):
            _softmax_rows(s_ref.at[slot, h], p_ref.at[h], al_ref.at[h], m_ref.at[h], l_ref.at[h], SOFTMAX_CHUNK)
        for h in range(hk):
            acc_ref[h] = acc_ref[h] * al_ref[h] + _dot(p_ref[h], vs_ref[pl.ds(k0, tk), kv_cols(h)])

    _two_slot_pipeline((q0 + tq - 1) // tk + 1, produce, consume)

    gt = _sigmoid(ga_ref[...].astype(F32))
    for h in range(hk):
        for a in range(g):
            cols = slice((h * g + a) * HEAD_DIM, (h * g + a + 1) * HEAD_DIM)
            head = slice(a * tq, (a + 1) * tq)
            lane = (h * g + a) * 3
            o_sel = acc_ref[h, head, :] / l_ref[h, head, :]
            mix = (gt[:, lane:lane + 1] * ocmp_ref[:, cols].astype(F32) + gt[:, lane + 1:lane + 2] * o_sel
                   + gt[:, lane + 2:lane + 3] * ow_ref[h, head, :])
            gate = _silu(gate_refs[h][:, a * HEAD_DIM:(a + 1) * HEAD_DIM].astype(F32))
            o_ref[:, cols] = (mix * gate).astype(o_ref.dtype)


def _nsa_attn(h, cmp_kv, cos2, sin2, batch, seq, tq, tk):
    hk, g = NSA_KV_HEADS, NSA_GROUP
    nq = seq // tq
    rows = g * tq
    wq = NSA_HEADS * HEAD_DIM
    wkv = hk * HEAD_DIM
    n_rows = seq // CMP_STRIDE
    n_sel_blk = seq // SEL_LEN
    assert WIN % tq == 0 and tk % SEL_LEN == 0 and seq % tk == 0 and seq >= WIN + tq and tk % tq == 0 and hk == 2
    assert n_sel_blk <= LANES and tq == LANES
    slab = lambda col: pl.BlockSpec((seq, wkv), lambda b, i: (b, col // hk))
    cmp_spec = lambda which: pl.BlockSpec((None, hk, n_rows, HEAD_DIM), lambda b, i: (which, b, 0, 0))
    return pl.pallas_call(
        functools.partial(_nsa_attn_kernel, tq=tq, tk=tk, n_sel_blk=n_sel_blk, n_sel=min(SEL_TOPK, n_sel_blk)),
        grid=(batch, nq),
        in_specs=[
            pl.BlockSpec((tq, wq), lambda b, i: (b * nq + i, EV_QA * LANES // wq)),
            cmp_spec(0), cmp_spec(1),
            slab(EV_KS), slab(EV_VS), slab(EV_KW), slab(EV_VW),
            pl.BlockSpec((tq, LANES), lambda b, i: (i, 0)),
            pl.BlockSpec((tq, LANES), lambda b, i: (i, 0)),
            pl.BlockSpec((tq, LANES), lambda b, i: (b * nq + i, EV_GA)),
            pl.BlockSpec((tq, g * HEAD_DIM), lambda b, i: (b * nq + i, EV_GATE_A // g)),
            pl.BlockSpec((tq, g * HEAD_DIM), lambda b, i: (b * nq + i, EV_GATE_A // g + 1)),
        ],
        out_specs=pl.BlockSpec((tq, wq), lambda b, i: (b * nq + i, 0)),
        out_shape=jax.ShapeDtypeStruct((batch * seq, wq), BF16),
        scratch_shapes=[
            pltpu.VMEM((hk, rows, LANES), F32), pltpu.VMEM((hk, rows, LANES), F32), pltpu.VMEM((hk, rows, LANES), F32),
            pltpu.VMEM((hk, rows, HEAD_DIM), F32), pltpu.VMEM((hk, rows, HEAD_DIM), BF16),
            pltpu.VMEM((2, hk, rows, tk), F32), pltpu.VMEM((hk, rows, tk), BF16),
            pltpu.VMEM((hk, rows, WIN + tq), F32), pltpu.VMEM((hk, rows, WIN + tq), BF16),
            pltpu.VMEM((hk, rows, HEAD_DIM), F32),
            pltpu.VMEM((tq, wq), BF16), pltpu.VMEM((hk, tq, LANES), BF16),
        ],
        compiler_params=_cparams(("arbitrary", "arbitrary")),
        name="nsa_attn",
    )(h, cmp_kv, cmp_kv, h, h, h, h, cos2, sin2, h, h, h)


def _sb_kernel(q_ref, k_ref, v_ref, gate_ref, tri_ref, o_ref, later_ref, acc_ref, z_ref, zs_ref, sp16_ref, sp0_ref,
               after_ref, a_ref, *, t):
    i = pl.program_id(2)
    n_q = pl.num_programs(2)
    chunk = SOFTMAX_CHUNK
    nh = acc_ref.shape[0]
    tb = tri_ref.shape[0]
    nb = t // tb
    later_ref[...] = jnp.zeros(later_ref.shape, F32)
    acc_ref[...] = jnp.zeros(acc_ref.shape, F32)
    head_cols = lambda h: slice(h * HEAD_DIM, (h + 1) * HEAD_DIM)

    def logits(slot, qi, kj, diagonal):
        q0 = pl.multiple_of(qi * t, t)
        k0 = pl.multiple_of(kj * t, t)
        for h in range(nh):
            z = _dot_nt(q_ref[pl.ds(q0, t), head_cols(h)], k_ref[pl.ds(k0, t), head_cols(h)])
            if diagonal:
                strict = lax.broadcasted_iota(jnp.int32, (t, t), 1) < lax.broadcasted_iota(jnp.int32, (t, t), 0)
                z = jnp.where(strict, z, NEG)
            z_ref[slot, h] = z

    def produce_first(slot, qi):
        qi = jnp.minimum(qi, n_q - 1)
        logits(slot, qi, qi, True)

    def produce(slot, k):
        logits(slot, i, i - k, False)

    def consume(slot, k):
        k0 = pl.multiple_of((i - k) * t, t)
        for h in range(nh):
            for r in range(t // chunk):
                sl = slice(r * chunk, (r + 1) * chunk)
                z = z_ref[slot, h, sl, :]
                sp = jnp.maximum(z, 0.0) + jnp.log(1.0 + jnp.exp2(jnp.minimum(z, -z))) * LOG2E
                zs_ref[h, sl, :] = z - sp
                sp16_ref[h, sl, :] = sp.astype(BF16)
                sp0_ref[h, sl, :] = jnp.concatenate([sp[:, b * tb:b * tb + LANES] for b in range(nb)], axis=1)
        for h in range(nh):
            for b in range(nb):
                bc = slice(b * tb, (b + 1) * tb)
                after_ref[h, :, bc] = _dot(sp16_ref[h, :, bc], tri_ref[...])
        for h in range(nh):
            for r in range(t // chunk):
                sl = slice(r * chunk, (r + 1) * chunk)
                run = later_ref[h, sl, :]
                for b in reversed(range(nb)):
                    bc = slice(b * tb, (b + 1) * tb)
                    after = after_ref[h, sl, bc]
                    a_ref[h, sl, bc] = jnp.exp2(zs_ref[h, sl, bc] - after - _lane_tile(run, tb // LANES)).astype(BF16)
                    run = run + (after[:, 0:1] + sp0_ref[h, sl, b * LANES:b * LANES + 1])
                later_ref[h, sl, :] = run
            acc_ref[h] += _dot(a_ref[h], v_ref[pl.ds(k0, t), head_cols(h)])

    _carried_pipeline(i, [j + 1 for j in range(q_ref.shape[0] // t)], produce_first, produce, consume)
    for h in range(nh):
        o_ref[:, head_cols(h)] = (acc_ref[h] * _silu(gate_ref[:, head_cols(h)].astype(F32))).astype(o_ref.dtype)


def _stick_breaking(h, batch, seq, t, nh, tb):
    nq = seq // t
    w = nh * HEAD_DIM
    assert SB_HEADS % nh == 0 and t % tb == 0
    tri = (lax.broadcasted_iota(jnp.int32, (tb, tb), 0) > lax.broadcasted_iota(jnp.int32, (tb, tb), 1)).astype(BF16)
    col = lambda first_block: first_block // nh
    return pl.pallas_call(
        functools.partial(_sb_kernel, t=t),
        grid=(batch, SB_HEADS // nh, nq),
        in_specs=[
            pl.BlockSpec((seq, w), lambda b, h_, i: (b, col(EV_QB) + h_)),
            pl.BlockSpec((seq, w), lambda b, h_, i: (b, col(EV_KB) + h_)),
            pl.BlockSpec((seq, w), lambda b, h_, i: (b, col(EV_VB) + h_)),
            pl.BlockSpec((t, w), lambda b, h_, i: (b * nq + i, col(EV_GATE_B) + h_)),
            pl.BlockSpec((tb, tb), lambda b, h_, i: (0, 0)),
        ],
        out_specs=pl.BlockSpec((t, w), lambda b, h_, i: (b * nq + i, h_)),
        out_shape=jax.ShapeDtypeStruct((batch * seq, SB_HEADS * HEAD_DIM), BF16),
        scratch_shapes=[
            pltpu.VMEM((nh, t, LANES), F32), pltpu.VMEM((nh, t, HEAD_DIM), F32), pltpu.VMEM((2, nh, t, t), F32),
            pltpu.VMEM((nh, t, t), F32), pltpu.VMEM((nh, t, t), BF16), pltpu.VMEM((nh, t, (t // tb) * LANES), F32),
            pltpu.VMEM((nh, t, t), F32), pltpu.VMEM((nh, t, t), BF16),
        ],
        compiler_params=_cparams(("arbitrary", "arbitrary", "arbitrary")),
        name="stick_breaking",
    )(h, h, h, h, tri)


def _diff_kernel(q_ref, k_ref, v_ref, gate_ref, gn_ref, lq1_ref, lk1_ref, lq2_ref, lk2_ref, o_ref,
                 m_ref, l_ref, acc_ref, s_ref, p_ref, al_ref, *, t, tk, lambda_init):
    d = HEAD_DIM
    i = pl.program_id(2)
    n_q = q_ref.shape[0] // t
    n_str = m_ref.shape[0]
    _softmax_init(m_ref, l_ref)
    acc_ref[...] = jnp.zeros(acc_ref.shape, F32)
    steps = [(j * t + t - 1) // tk + 1 for j in range(n_q)]

    def scores(slot, qi, kj, diagonal):
        q0 = pl.multiple_of(qi * t, t)
        k0 = pl.multiple_of(kj * tk, tk)
        for u in range(n_str):
            s = _dot_nt(q_ref[pl.ds(q0, t), u * d:(u + 1) * d], k_ref[pl.ds(k0, tk), u * d:(u + 1) * d])
            if diagonal:
                key = lax.broadcasted_iota(jnp.int32, (t, tk), 1)
                row = lax.broadcasted_iota(jnp.int32, (t, tk), 0)
                causal = key <= row if tk == t else k0 + key <= q0 + row
                s = jnp.where(causal, s, NEG)
            s_ref[slot, u] = s

    def last_tile(qi):
        return (qi * t + t - 1) // tk

    def produce_first(slot, qi):
        qi = jnp.minimum(qi, n_q - 1)
        scores(slot, qi, last_tile(qi), True)

    def produce(slot, k):
        scores(slot, i, k - 1, False)

    def consume(slot, k):
        k0 = pl.multiple_of(jnp.where(k == 0, last_tile(i), k - 1) * tk, tk)
        for u in range(n_str):
            _softmax_rows(s_ref.at[slot, u], p_ref.at[u], al_ref.at[u], m_ref.at[u], l_ref.at[u], DIFF_CHUNK)
        for u in range(n_str):
            vv = v_ref[pl.ds(k0, tk), (u // 2) * 2 * d:(u // 2 + 1) * 2 * d]
            acc_ref[u] = acc_ref[u] * _lane_tile(al_ref[u], 2) + _dot(p_ref[u], vv)

    _carried_pipeline(i, steps, produce_first, produce, consume)
    lam = (jnp.exp(jnp.sum(lq1_ref[...] * lk1_ref[...], axis=-1, keepdims=True))
           - jnp.exp(jnp.sum(lq2_ref[...] * lk2_ref[...], axis=-1, keepdims=True)) + lambda_init)
    for hh in range(n_str // 2):
        cols = slice(hh * 2 * d, (hh + 1) * 2 * d)
        inv = [1.0 / _lane_tile(l_ref[2 * hh + c], 2) for c in range(2)]
        o = acc_ref[2 * hh] * inv[0] - lam * (acc_ref[2 * hh + 1] * inv[1])
        o = o * lax.rsqrt(jnp.mean(o * o, axis=-1, keepdims=True) + RMS_EPS)
        o = o * gn_ref[hh:hh + 1, :] * (1.0 - lambda_init)
        o_ref[:, cols] = (o * _silu(gate_ref[:, cols].astype(F32))).astype(o_ref.dtype)


def _diff_attn(h, gn_g, lq1, lk1, lq2, lk2, lambda_init, batch, seq, t, tk, hps):
    nh = DIFF_HEADS
    nq = seq // t
    w = hps * 2 * HEAD_DIM
    ng = nh // hps
    assert nh % hps == 0 and seq % tk == 0 and tk % t == 0
    vec = pl.BlockSpec((1, HEAD_DIM), lambda b, h_, i: (0, 0))
    r = lambda a: a.reshape(1, HEAD_DIM)
    return pl.pallas_call(
        functools.partial(_diff_kernel, t=t, tk=tk, lambda_init=lambda_init),
        grid=(batch, ng, nq),
        in_specs=[
            pl.BlockSpec((seq, w), lambda b, h_, i: (b, h_)),
            pl.BlockSpec((seq, w), lambda b, h_, i: (b, ng + h_)),
            pl.BlockSpec((seq, w), lambda b, h_, i: (b, 2 * ng + h_)),
            pl.BlockSpec((t, w), lambda b, h_, i: (b * nq + i, 3 * ng + h_)),
            pl.BlockSpec((None, hps, 2 * HEAD_DIM), lambda b, h_, i: (h_, 0, 0)),
            vec, vec, vec, vec,
        ],
        out_specs=pl.BlockSpec((t, w), lambda b, h_, i: (b * nq + i, h_)),
        out_shape=jax.ShapeDtypeStruct((batch * seq, nh * 2 * HEAD_DIM), BF16),
        scratch_shapes=[
            pltpu.VMEM((2 * hps, t, LANES), F32), pltpu.VMEM((2 * hps, t, LANES), F32),
            pltpu.VMEM((2 * hps, t, 2 * HEAD_DIM), F32), pltpu.VMEM((2, 2 * hps, t, tk), F32),
            pltpu.VMEM((2 * hps, t, tk), BF16), pltpu.VMEM((2 * hps, t, LANES), F32),
        ],
        compiler_params=_cparams(("arbitrary", "arbitrary", "arbitrary")),
        name="diff_attn",
    )(h, h, h, h, gn_g.reshape(ng, hps, 2 * HEAD_DIM), r(lq1), r(lk1), r(lq2), r(lk2))


def _rope_tables(seq):
    pos = jnp.arange(seq, dtype=F32)
    inv = ROPE_THETA ** (-jnp.arange(0, HEAD_DIM, 2, dtype=F32) / HEAD_DIM)
    ang = pos[:, None] * inv[None, :]
    cos, sin = jnp.cos(ang), jnp.sin(ang)
    return jnp.concatenate([cos, cos], axis=-1), jnp.concatenate([-sin, sin], axis=-1)


def _even_weights_nk(w_in, tn):
    tail0 = EV_GA * LANES + 3 * NSA_HEADS
    w_t = w_in.T
    n_head, n_tail = EV_GATE_A * LANES // tn, (w_t.shape[0] - tail0) // tn
    assert EV_GATE_A * LANES % tn == 0 and (n_head + n_tail) * tn == EV_WIDTH
    return [w_t, w_t[tail0:].astype(BF16)], [(0, t) for t in range(n_head)] + [(1, t) for t in range(n_tail)]


def _even_tile_modes(tn):
    modes = [(False, False)] * (EV_WIDTH // LANES)
    for c in range(EV_QA, EV_QA + NSA_HEADS):
        modes[c] = (False, True)
    for c in list(range(EV_KS, EV_KS + NSA_KV_HEADS)) + list(range(EV_KW, EV_KW + NSA_KV_HEADS)):
        modes[c] = (True, False)
    for c in range(EV_QB, EV_QB + SB_HEADS):
        modes[c] = (False, True)
    per = tn // LANES
    return tuple(tuple(modes[i * per:(i + 1) * per]) for i in range(len(modes) // per))


def _odd_tile_modes(tn):
    nh = 2 * DIFF_HEADS
    modes = [(True, True)] * nh + [(True, False)] * nh + [(False, False)] * (2 * nh)
    per = tn // LANES
    return tuple(tuple(modes[i * per:(i + 1) * per]) for i in range(len(modes) // per))


TM_PROJ, TN_PROJ = 1024, 1024
TM_OUT = 512
TQ_NSA = 128
TK_NSA = 512
T_SB = 512
SB_HEADS_PER_STEP = 2
SB_TRI_BLOCK = 256
T_DIFF = 512
TK_DIFF = 512
DIFF_CHUNK = SOFTMAX_CHUNK
DIFF_HEADS_PER_STEP = 1


def _even_layer(x2, batch, seq, w_in, pe_k, pe_v, w1k, w2k, w1v, w2v, w_out, ln_g, ln_b, cos2, sin2, side):
    d = HEAD_DIM
    tm = min(TM_PROJ, seq)
    ws, tile_src = _even_weights_nk(w_in, TN_PROJ)
    h = _inproj(x2, ws, tile_src, True, cos2, sin2, _even_tile_modes(TN_PROJ), tm, TN_PROJ)
    pe = jnp.stack([pe_k, pe_v])
    w1 = jnp.stack([w1k, w1v]).reshape(2, CMP_LEN, d, d).astype(BF16)
    w2 = jnp.stack([w2k, w2v]).astype(BF16)
    cmp_kv = _compress(h, pe, w1, w2, batch, seq)
    o_a = _nsa_attn(h, cmp_kv, cos2, sin2, batch, seq, TQ_NSA, min(TK_NSA, seq))
    o_b = _stick_breaking(h, batch, seq, min(T_SB, seq), SB_HEADS_PER_STEP, SB_TRI_BLOCK)
    return _outproj_ln([o_a, o_b], w_out.astype(BF16), x2, ln_g, ln_b, TM_OUT, side)


def _odd_layer(x2, batch, seq, w_in_bf16, lq1, lk1, lq2, lk2, gn_g, w_out_bf16, ln_g, ln_b, lambda_init, cos2, sin2):
    tm = min(TM_PROJ, seq)
    tile_src = [(0, t) for t in range(w_in_bf16.shape[1] // TN_PROJ)]
    h = _inproj(x2, [w_in_bf16], tile_src, False, cos2, sin2, _odd_tile_modes(TN_PROJ), tm, TN_PROJ)
    o = _diff_attn(h, gn_g, lq1, lk1, lq2, lk2, lambda_init, batch, seq, min(T_DIFF, seq), min(TK_DIFF, seq),
                   DIFF_HEADS_PER_STEP)
    return _outproj_ln([o], w_out_bf16, x2, ln_g, ln_b, TM_OUT)[0]


def kernel(x, ev_w_in, ev_pe_k, ev_pe_v, ev_w1_k, ev_w2_k, ev_w1_v, ev_w2_v, ev_w_out, ev_ln_g, ev_ln_b,
           od_w_in, od_lq1, od_lk1, od_lq2, od_lk2, od_gn_g, od_w_out, od_ln_g, od_ln_b):
    batch, seq, dm = x.shape
    cos2, sin2 = _rope_tables(seq)
    x2 = x.reshape(batch * seq, dm)
    odd_bf16 = {}
    for layer in range(DEPTH):
        i = layer // 2
        if layer % 2 == 0:
            side = (od_w_in[i], od_w_out[i]) if layer + 1 < DEPTH else ()
            x2, cast = _even_layer(x2, batch, seq, ev_w_in[i], ev_pe_k[i], ev_pe_v[i], ev_w1_k[i], ev_w2_k[i],
                                   ev_w1_v[i], ev_w2_v[i], ev_w_out[i], ev_ln_g[i], ev_ln_b[i], cos2, sin2, side)
            odd_bf16[i] = cast
        else:
            lambda_init = 0.8 - 0.6 * math.exp(-0.3 * layer)
            w_in_bf16, w_out_bf16 = odd_bf16[i]
            x2 = _odd_layer(x2, batch, seq, w_in_bf16, od_lq1[i], od_lk1[i], od_lq2[i], od_lk2[i], od_gn_g[i],
                            w_out_bf16, od_ln_g[i], od_ln_b[i], lambda_init, cos2, sin2)
    return x2.reshape(batch, seq, dm)
```
